```python
import jax, jax.numpy as jnp
from jax import lax
import numpy as np

D_MODEL = 1024
BATCH = 4
SEQ = 4096
DEPTH = 2
DEC_BATCH = 128
DEC_SEQ = 1
PAST_LEN = 2048
PAGE_SIZE = 128

N_HEADS = 16
N_KV = 4
HEAD_DIM = 64
Q_PER_KV = N_HEADS // N_KV
NSA_WIDTH = N_HEADS * HEAD_DIM
KV_COLS = 2 * N_KV * HEAD_DIM
CMP_BLOCK = 64
SEL_BLOCK = CMP_BLOCK
N_SEL = 8
WINDOW = 512
Q_BLOCK = 64
SC_WIDTH = 512
SC_K = 3
LRU_WIDTH = 512
LRU_HEADS = 8
LRU_BW = LRU_WIDTH // LRU_HEADS
LRU_K = 4
LRU_C = 8.0
D_FF = -(-8 * D_MODEL // (3 * 256)) * 256

SPLIT_SIZES = (NSA_WIDTH, KV_COLS, KV_COLS, KV_COLS, 3 * N_HEADS,
               SC_WIDTH, SC_WIDTH, SC_WIDTH, LRU_WIDTH, LRU_WIDTH, 3 * D_MODEL)
SPLIT_POINTS = tuple(int(v) for v in np.cumsum(SPLIT_SIZES)[:-1])
N_IN = sum(SPLIT_SIZES)
SCALE = HEAD_DIM ** -0.5
EPS = 1e-6
NEG = -1e30
TINY = 1e-20
FORCE = 1e6

kernel_name = 'nsa_shortconv_rglru_hybrid_step'


def _rmsnorm(x, g):
    xf = x.astype(jnp.float32)
    y = xf * lax.rsqrt(jnp.mean(xf * xf, axis=-1, keepdims=True) + EPS)
    return (y * g.astype(jnp.float32)).astype(x.dtype)


def _masked_probs(s, mask):
    s = jnp.where(mask, s, NEG)
    m = jnp.max(s, axis=-1, keepdims=True)
    p = jnp.where(mask, jnp.exp(s - m), 0.0)
    return p / jnp.maximum(jnp.sum(p, axis=-1, keepdims=True), TINY)


def _compress(k, w):
    B, L, G, D = k.shape
    kb = k.reshape(B, L // CMP_BLOCK, CMP_BLOCK, G, D)
    return jnp.einsum('bnjgd,jd->bngd', kb, w)


def _to_blocks(k):
    B, L, G, D = k.shape
    return k.reshape(B, L // SEL_BLOCK, SEL_BLOCK, G, D).transpose(0, 3, 1, 2, 4)


def _cmp_and_select(qg, q_pos, k_c, v_c, w_ck, w_cv):
    kc = _compress(k_c, w_ck)
    vc = _compress(v_c, w_cv)
    nblk = kc.shape[1]
    s = jnp.einsum('btgrd,bngd->bgrtn', qg, kc).astype(jnp.float32) * SCALE
    blk = jnp.arange(nblk)
    mask = ((blk + 1) * CMP_BLOCK - 1)[None, :] <= q_pos[:, None]
    p = _masked_probs(s, mask)
    o = jnp.einsum('bgrtn,bngd->btgrd', p, vc)
    cur = (q_pos // SEL_BLOCK)[:, None]
    imp = jnp.sum(p, axis=2)
    imp = imp + FORCE * (blk == cur) + (0.5 * FORCE) * (blk == 0)[None, :]
    imp = jnp.where(blk[None, :] <= cur, imp, -FORCE)
    _, idx = lax.top_k(imp, min(N_SEL, nblk))
    return o, idx


def _sel_attn(qg, q_pos, idx, kb, vb):
    B, G, T, n = idx.shape
    M = n * SEL_BLOCK
    bi = jnp.arange(B)[:, None, None, None]
    gi = jnp.arange(G)[None, :, None, None]
    ks = kb[bi, gi, idx].reshape(B, G, T, M, HEAD_DIM)
    vs = vb[bi, gi, idx].reshape(B, G, T, M, HEAD_DIM)
    kpos = (idx[..., None] * SEL_BLOCK + jnp.arange(SEL_BLOCK)).reshape(B, G, T, M)[:, :, None]
    mask = kpos <= q_pos[:, None]
    s = jnp.einsum('btgrd,bgtmd->bgrtm', qg, ks).astype(jnp.float32) * SCALE
    p = _masked_probs(s, mask)
    return jnp.einsum('bgrtm,bgtmd->btgrd', p, vs)


def _win_attn(qg, q_pos, kw, vw, k_pos):
    s = jnp.einsum('btgrd,bkgd->bgrtk', qg, kw).astype(jnp.float32) * SCALE
    dlt = q_pos[:, None] - k_pos[None, :]
    mask = (dlt >= 0) & (dlt < WINDOW) & (k_pos >= 0)[None, :]
    p = _masked_probs(s, mask)
    return jnp.einsum('bgrtk,bkgd->btgrd', p, vw)


def _nsa_prompt(qg, pos, k_c, v_c, k_s, v_s, k_w, v_w, w_ck, w_cv):
    B, S, G, R, Dh = qg.shape
    o_c, idx = _cmp_and_select(qg, pos, k_c, v_c, w_ck, w_cv)
    kb, vb = _to_blocks(k_s), _to_blocks(v_s)
    pad = ((0, 0), (WINDOW, 0), (0, 0), (0, 0))
    kwp, vwp = jnp.pad(k_w, pad), jnp.pad(v_w, pad)
    span = WINDOW + Q_BLOCK

    def block(i):
        st = i * Q_BLOCK
        qb = lax.dynamic_slice_in_dim(qg, st, Q_BLOCK, axis=1)
        pb = st + jnp.arange(Q_BLOCK)
        ib = lax.dynamic_slice_in_dim(idx, st, Q_BLOCK, axis=2)
        o_s = _sel_attn(qb, pb, ib, kb, vb)
        kp = st - WINDOW + jnp.arange(span)
        o_w = _win_attn(qb, pb, lax.dynamic_slice_in_dim(kwp, st, span, axis=1),
                        lax.dynamic_slice_in_dim(vwp, st, span, axis=1), kp)
        return o_s, o_w

    o_s, o_w = lax.map(block, jnp.arange(S // Q_BLOCK))
    o_s = jnp.moveaxis(o_s, 0, 1).reshape(B, S, G, R, Dh)
    o_w = jnp.moveaxis(o_w, 0, 1).reshape(B, S, G, R, Dh)
    w_keep = min(WINDOW, S)
    win_state = jnp.stack([k_w[:, S - w_keep:], v_w[:, S - w_keep:]], axis=2)
    return o_c, o_s, o_w, win_state


def _make_nsa_sample(past, win_buf):
    def nsa_fn(qg, pos, k_c, v_c, k_s, v_s, k_w, v_w, w_ck, w_cv):
        T = qg.shape[1]
        P = past.shape[1]
        L = P + T
        pad = -(-L // CMP_BLOCK) * CMP_BLOCK - L

        def full(i, new):
            seq = jnp.concatenate([past[:, :, i], new], axis=1)
            return jnp.pad(seq, ((0, 0), (0, pad), (0, 0), (0, 0)))

        o_c, idx = _cmp_and_select(qg, pos, full(0, k_c), full(1, v_c), w_ck, w_cv)
        o_s = _sel_attn(qg, pos, idx, _to_blocks(full(2, k_s)), _to_blocks(full(3, v_s)))
        Wc = win_buf.shape[1]
        kw = jnp.concatenate([win_buf[:, :, 0], k_w], axis=1)
        vw = jnp.concatenate([win_buf[:, :, 1], v_w], axis=1)
        kp = (P - Wc) + jnp.arange(Wc + T)
        o_w = _win_attn(qg, pos, kw, vw, kp)
        win_state = jnp.stack([kw[:, T:], vw[:, T:]], axis=2)
        return o_c, o_s, o_w, win_state
    return nsa_fn


def _causal_conv(u, prev, w):
    K = w.shape[0]
    T = u.shape[1]
    ext = jnp.concatenate([prev.astype(u.dtype), u], axis=1)
    y = ext[:, 0:T] * w[0]
    for k in range(1, K):
        y = y + ext[:, k:k + T] * w[k]
    return y, ext[:, T:]


def _lru_step(hc, ab):
    a_t, b_t = ab
    hn = a_t * hc + b_t
    return hn, hn


def _layer_mixers(h, pos, lp, nsa_fn, sc_prev, lru_conv_prev, lru_h0):
    B, T, _ = h.shape
    (q, kv_c, kv_s, kv_w, g_nsa, b_sc, c_sc, x_sc, x_lru, y_lru, g_mix) = jnp.split(
        h @ lp['w_in'], SPLIT_POINTS, axis=-1)
    qg = q.reshape(B, T, N_KV, Q_PER_KV, HEAD_DIM)
    kv_c = kv_c.reshape(B, T, 2, N_KV, HEAD_DIM)
    kv_s = kv_s.reshape(B, T, 2, N_KV, HEAD_DIM)
    kv_w = kv_w.reshape(B, T, 2, N_KV, HEAD_DIM)
    o_c, o_s, o_w, win_state = nsa_fn(qg, pos, kv_c[:, :, 0], kv_c[:, :, 1], kv_s[:, :, 0], kv_s[:, :, 1],
                                      kv_w[:, :, 0], kv_w[:, :, 1], lp['w_cmp_k'], lp['w_cmp_v'])
    g = jax.nn.sigmoid(g_nsa.reshape(B, T, N_KV, Q_PER_KV, 3, 1))
    o_nsa = (g[..., 0, :] * o_c + g[..., 1, :] * o_s + g[..., 2, :] * o_w).reshape(B, T, NSA_WIDTH)
    u_nsa = o_nsa @ lp['w_nsa_out']
    conv_u, sc_state = _causal_conv(c_sc * x_sc, sc_prev, lp['w_sc_conv'])
    u_sc = (b_sc * conv_u) @ lp['w_sc_out']
    xc, lru_conv_state = _causal_conv(x_lru, lru_conv_prev, lp['w_lru_conv'])
    xc = xc + lp['b_lru_conv']
    gates = jnp.einsum('bthi,khij->kbthj', xc.reshape(B, T, LRU_HEADS, LRU_BW),
                       lp['w_lru_gate']).reshape(2, B, T, LRU_WIDTH) + lp['b_lru_gate'][:, None, None, :]
    gates = jax.nn.sigmoid(gates.astype(jnp.float32))
    log_a = -LRU_C * gates[0] * jax.nn.softplus(-lp['lru_lambda'].astype(jnp.float32))
    reset = (pos == 0)[None, :, None]
    a = jnp.where(reset, 0.0, jnp.exp(log_a))
    mult = jnp.where(reset, 1.0, jnp.sqrt(-jnp.expm1(2.0 * log_a)))
    bx = mult * gates[1] * xc.astype(jnp.float32)
    h_last, hs = lax.scan(_lru_step, lru_h0.astype(jnp.float32),
                          (jnp.swapaxes(a, 0, 1), jnp.swapaxes(bx, 0, 1)))
    hs = jnp.swapaxes(hs, 0, 1).astype(h.dtype)
    u_lru = (hs * jax.nn.gelu(y_lru)) @ lp['w_lru_out']
    gm = jax.nn.sigmoid(g_mix.reshape(B, T, 3, D_MODEL))
    out = (gm[:, :, 0] * u_nsa + gm[:, :, 1] * u_sc + gm[:, :, 2] * u_lru) @ lp['w_o']
    kv_rows = jnp.concatenate([kv_c, kv_s], axis=2)
    return out, (kv_rows, win_state, sc_state, lru_conv_state, h_last)


def _swiglu(h, w_in, w_out):
    gt, up = jnp.split(h @ w_in, 2, axis=-1)
    return (jax.nn.silu(gt) * up) @ w_out


def _stack_states(states):
    return tuple(jnp.stack([s[i] for s in states]) for i in range(5))


def setup_inputs(seed: int = 0) -> dict:
    key = jax.random.key(seed)
    k = jax.random.split(key, 32)
    f32 = jnp.float32

    def nrm(kk, shape, fan):
        return jax.random.normal(kk, shape, f32) * (fan ** -0.5)

    n_pages = PAST_LEN // PAGE_SIZE
    n_used = DEC_BATCH * n_pages
    n_pool = n_used + max(1, n_used // 4)
    win_len = min(WINDOW, PAST_LEN)
    page_table = jax.random.permutation(k[0], n_pool)[:n_used].reshape(DEC_BATCH, n_pages).astype(jnp.int32)
    u = jax.random.uniform(k[1], (DEPTH, LRU_WIDTH), f32, 0.9, 0.999)
    return {
        'x_prompt': jax.random.normal(k[2], (BATCH, SEQ, D_MODEL), f32),
        'x_sample': jax.random.normal(k[3], (DEC_BATCH, DEC_SEQ, D_MODEL), f32),
        'cache_kv': jax.random.normal(k[4], (DEPTH, n_pool, PAGE_SIZE, 4, N_KV, HEAD_DIM), f32),
        'cache_win': jax.random.normal(k[5], (DEPTH, DEC_BATCH, win_len, 2, N_KV, HEAD_DIM), f32),
        'state_sconv': jax.random.normal(k[6], (DEPTH, DEC_BATCH, SC_K - 1, SC_WIDTH), f32),
        'state_lru_conv': jax.random.normal(k[7], (DEPTH, DEC_BATCH, LRU_K - 1, LRU_WIDTH), f32),
        'state_lru_h': 0.5 * jax.random.normal(k[8], (DEPTH, DEC_BATCH, LRU_WIDTH), f32),
        'page_table': page_table,
        'norm1_g': 1.0 + 0.1 * jax.random.normal(k[9], (DEPTH, D_MODEL), f32),
        'w_in': nrm(k[10], (DEPTH, D_MODEL, N_IN), D_MODEL),
        'w_cmp_k': (1.0 + 0.1 * jax.random.normal(k[11], (DEPTH, CMP_BLOCK, HEAD_DIM), f32)) * CMP_BLOCK ** -0.5,
        'w_cmp_v': (1.0 + 0.1 * jax.random.normal(k[12], (DEPTH, CMP_BLOCK, HEAD_DIM), f32)) * CMP_BLOCK ** -0.5,
        'w_nsa_out': nrm(k[13], (DEPTH, NSA_WIDTH, D_MODEL), NSA_WIDTH),
        'w_sc_conv': nrm(k[14], (DEPTH, SC_K, SC_WIDTH), SC_K),
        'w_sc_out': nrm(k[15], (DEPTH, SC_WIDTH, D_MODEL), SC_WIDTH),
        'w_lru_conv': nrm(k[16], (DEPTH, LRU_K, LRU_WIDTH), LRU_K),
        'b_lru_conv': 0.01 * jax.random.normal(k[17], (DEPTH, LRU_WIDTH), f32),
        'w_lru_gate': nrm(k[18], (DEPTH, 2, LRU_HEADS, LRU_BW, LRU_BW), LRU_BW),
        'b_lru_gate': 0.01 * jax.random.normal(k[19], (DEPTH, 2, LRU_WIDTH), f32),
        'lru_lambda': jnp.log(u / (1.0 - u)),
        'w_lru_out': nrm(k[20], (DEPTH, LRU_WIDTH, D_MODEL), LRU_WIDTH),
        'w_o': nrm(k[21], (DEPTH, D_MODEL, D_MODEL), D_MODEL),
        'norm2_g': 1.0 + 0.1 * jax.random.normal(k[22], (DEPTH, D_MODEL), f32),
        'w_ffn_in': nrm(k[23], (DEPTH, D_MODEL, 2 * D_FF), D_MODEL),
        'w_ffn_out': nrm(k[24], (DEPTH, D_FF, D_MODEL), D_FF),
        'final_g': 1.0 + 0.1 * jax.random.normal(k[25], (D_MODEL,), f32),
    }


def reference(x_prompt, x_sample, cache_kv, cache_win, state_sconv, state_lru_conv, state_lru_h, page_table,
              norm1_g, w_in, w_cmp_k, w_cmp_v, w_nsa_out, w_sc_conv, w_sc_out, w_lru_conv, b_lru_conv,
              w_lru_gate, b_lru_gate, lru_lambda, w_lru_out, w_o, norm2_g, w_ffn_in, w_ffn_out, final_g):
    B, S, _ = x_prompt.shape
    DB, T, _ = x_sample.shape
    P = page_table.shape[1] * cache_kv.shape[2]
    pos_p = jnp.arange(S, dtype=jnp.int32)
    pos_s = P + jnp.arange(T, dtype=jnp.int32)
    xp, xs = x_prompt, x_sample
    st_p, st_s = [], []
    for l in range(DEPTH):
        lp = {'w_in': w_in[l], 'w_cmp_k': w_cmp_k[l], 'w_cmp_v': w_cmp_v[l], 'w_nsa_out': w_nsa_out[l],
              'w_sc_conv': w_sc_conv[l], 'w_sc_out': w_sc_out[l], 'w_lru_conv': w_lru_conv[l],
              'b_lru_conv': b_lru_conv[l], 'w_lru_gate': w_lru_gate[l], 'b_lru_gate': b_lru_gate[l],
              'lru_lambda': lru_lambda[l], 'w_lru_out': w_lru_out[l], 'w_o': w_o[l]}
        mp, sp = _layer_mixers(_rmsnorm(xp, norm1_g[l]), pos_p, lp, _nsa_prompt,
                               jnp.zeros((B, SC_K - 1, SC_WIDTH), xp.dtype),
                               jnp.zeros((B, LRU_K - 1, LRU_WIDTH), xp.dtype),
                               jnp.zeros((B, LRU_WIDTH), jnp.float32))
        xp = xp + mp
        xp = xp + _swiglu(_rmsnorm(xp, norm2_g[l]), w_ffn_in[l], w_ffn_out[l])
        past = cache_kv[l][page_table].reshape(DB, P, 4, N_KV, HEAD_DIM)
        ms, ss = _layer_mixers(_rmsnorm(xs, norm1_g[l]), pos_s, lp, _make_nsa_sample(past, cache_win[l]),
                               state_sconv[l], state_lru_conv[l], state_lru_h[l])
        xs = xs + ms
        xs = xs + _swiglu(_rmsnorm(xs, norm2_g[l]), w_ffn_in[l], w_ffn_out[l])
        st_p.append(sp)
        st_s.append(ss)
    y_prompt = _rmsnorm(xp, final_g)
    y_sample = _rmsnorm(xs, final_g)
    kv_p, win_p, sc_p, lc_p, lh_p = _stack_states(st_p)
    kv_s, win_s, sc_s, lc_s, lh_s = _stack_states(st_s)
    return (y_prompt, y_sample, kv_p, win_p, sc_p, lc_p, lh_p, kv_s, win_s, sc_s, lc_s, lh_s)
```

```python
import functools

import jax
import jax.numpy as jnp
from jax import lax
from jax.experimental import pallas as pl
from jax.experimental.pallas import tpu as pltpu

N_HEADS = 16
N_KV = 4
HEAD_DIM = 64
Q_PER_KV = N_HEADS // N_KV
CMP_BLOCK = 64
SEL_BLOCK = CMP_BLOCK
N_SEL = 8
WINDOW = 512
SC_K = 3
LRU_K = 4
LRU_HEADS = 8
LRU_C = 8.0
SCALE = HEAD_DIM ** -0.5
EPS = 1e-6
NEG = -1e30
TINY = 1e-20
FORCE = 1e6

LANES = 128
SUBLANES = 8
VMEM_LIMIT = 56 * 1024 * 1024

F32 = jnp.float32
BF16 = jnp.bfloat16
_NT = (((1,), (1,)), ((), ()))


def _params(sem):
    return pltpu.CompilerParams(dimension_semantics=sem, vmem_limit_bytes=VMEM_LIMIT)


def _const_spec(shape):
    nd = len(shape)
    return pl.BlockSpec(shape, lambda *_: (0,) * nd, pipeline_mode=pl.Buffered(1))


def _rms(x, g):
    y = x * lax.rsqrt(jnp.mean(x * x, axis=-1, keepdims=True) + EPS)
    return y * g


def _dot(a, b):
    return jnp.dot(a, b, preferred_element_type=F32)


def _qkv_kernel(*refs, compress):
    if compress:
        (x_ref, g_ref, wq_ref, wkp_ref, wkv_ref, wg_ref, wc_ref,
         q_ref, kvp_ref, kvr_ref, kvw_ref, gate_ref, cmp_ref) = refs
    else:
        (x_ref, g_ref, wq_ref, wkp_ref, wkv_ref, wg_ref,
         q_ref, kvp_ref, kvr_ref, kvw_ref, gate_ref) = refs
    h = _rms(x_ref[...], g_ref[...]).astype(BF16)
    q_ref[...] = (_dot(h, wq_ref[...]) * SCALE).astype(BF16)
    kvp = _dot(h, wkp_ref[...])
    kvp_ref[...] = kvp.astype(BF16)
    kvr = _dot(h, wkv_ref[...])
    n_rows = kvr_ref.shape[1]
    kvr_ref[...] = kvr[:, :n_rows]
    kvw_ref[...] = kvr[:, n_rows:]
    gate_ref[...] = jax.nn.sigmoid(_dot(h, wg_ref[...]))
    if compress:
        tm = x_ref.shape[0]
        width = wc_ref.shape[1]
        blocks = kvp[:, :width].reshape(tm // CMP_BLOCK, CMP_BLOCK, width)
        cmp_ref[...] = jnp.sum(blocks * wc_ref[...][None], axis=1)


def _qkv_proj(x, g, wq, wkp, wkv, wg, wc, tm):
    n, d = x.shape
    compress = wc is not None
    n_rows = 2 * 2 * N_KV * HEAD_DIM
    n_win = wkv.shape[1] - n_rows
    row = lambda i: (i, 0)
    in_specs = [pl.BlockSpec((tm, d), row), _const_spec(g.shape), _const_spec(wq.shape),
                _const_spec(wkp.shape), _const_spec(wkv.shape), _const_spec(wg.shape)]
    args = [x, g, wq, wkp, wkv, wg]
    out_shape = [jax.ShapeDtypeStruct((n, wq.shape[1]), BF16),
                 jax.ShapeDtypeStruct((n, wkp.shape[1]), BF16),
                 jax.ShapeDtypeStruct((n, n_rows), F32),
                 jax.ShapeDtypeStruct((n, n_win), F32),
                 jax.ShapeDtypeStruct((n, wg.shape[1]), F32)]
    out_specs = [pl.BlockSpec((tm, wq.shape[1]), row), pl.BlockSpec((tm, wkp.shape[1]), row),
                 pl.BlockSpec((tm, n_rows), row), pl.BlockSpec((tm, n_win), row),
                 pl.BlockSpec((tm, wg.shape[1]), row)]
    if compress:
        in_specs.append(_const_spec(wc.shape))
        args.append(wc)
        out_shape.append(jax.ShapeDtypeStruct((n // CMP_BLOCK, wc.shape[1]), F32))
        out_specs.append(pl.BlockSpec((tm // CMP_BLOCK, wc.shape[1]), row))
    return pl.pallas_call(
        functools.partial(_qkv_kernel, compress=compress),
        grid=(n // tm,), in_specs=in_specs, out_specs=out_specs, out_shape=out_shape,
        compiler_params=_params(("arbitrary",)), name="qkv_proj")(*args)


def _head_pad(qf, r, lane):
    pair = qf[:, (r // 2) * LANES:(r // 2 + 1) * LANES]
    if r % 2 == 1:
        pair = pltpu.roll(pair, HEAD_DIM, axis=1)
    return jnp.where(lane < HEAD_DIM, pair, 0.0)


def _cmp_topk_kernel(q_ref, cmp_ref, gate_ref, oc_ref, sel_ref, *, n_sel):
    tq = q_ref.shape[0]
    nblk = cmp_ref.shape[0]
    t0 = pl.program_id(2) * tq
    kcb = cmp_ref[...].astype(BF16)
    qf = q_ref[...].astype(F32)
    lane = lax.broadcasted_iota(jnp.int32, (tq, LANES), 1)
    tpos = t0 + lax.broadcasted_iota(jnp.int32, (tq, nblk), 0)
    blk = lax.broadcasted_iota(jnp.int32, (tq, nblk), 1)
    valid = (blk + 1) * CMP_BLOCK - 1 <= tpos
    gt = gate_ref[...]
    imp = jnp.zeros((tq, nblk), F32)
    for r in range(Q_PER_KV):
        qp = _head_pad(qf, r, lane).astype(BF16)
        s = lax.dot_general(qp, kcb, _NT, preferred_element_type=F32)
        s = jnp.where(valid, s, NEG)
        m = jnp.max(s, axis=-1, keepdims=True)
        p = jnp.where(valid, jnp.exp(s - m), 0.0)
        p = p / jnp.maximum(jnp.sum(p, axis=-1, keepdims=True), TINY)
        imp = imp + p
        o = _dot(p.astype(BF16), kcb)
        oc_ref[:, r * HEAD_DIM:(r + 1) * HEAD_DIM] = gt[:, 3 * r:3 * r + 1] * o[:, HEAD_DIM:]
    cur = lax.shift_right_logical(tpos, 6)
    imp = imp + jnp.where(blk == cur, FORCE, 0.0) + jnp.where(blk == 0, 0.5 * FORCE, 0.0)
    imp = jnp.where(blk <= cur, imp, -FORCE)
    blkf = blk.astype(F32)
    sel = jnp.zeros((tq, nblk), F32)
    for _ in range(n_sel):
        mx = jnp.max(imp, axis=-1, keepdims=True)
        first = jnp.min(jnp.where(imp == mx, blkf, float(nblk)), axis=-1, keepdims=True)
        hit = blkf == first
        sel = jnp.where(hit, 1.0, sel)
        imp = jnp.where(hit, -jnp.inf, imp)
    sel_ref[...] = jnp.where(blk <= cur, sel, 0.0).astype(BF16)


def _cmp_topk(q, cmpk, gates, batch, seq, tq):
    n = q.shape[0]
    nblk = seq // CMP_BLOCK
    nq = seq // tq
    gw = Q_PER_KV * HEAD_DIM
    row_g = lambda b, g, i: (b * nq + i, g)
    return pl.pallas_call(
        functools.partial(_cmp_topk_kernel, n_sel=min(N_SEL, nblk)),
        grid=(batch, N_KV, nq),
        in_specs=[pl.BlockSpec((tq, gw), row_g),
                  pl.BlockSpec((nblk, LANES), lambda b, g, i: (b, g)),
                  pl.BlockSpec((tq, LANES), row_g)],
        out_specs=[pl.BlockSpec((tq, gw), row_g),
                   pl.BlockSpec((None, tq, nblk), lambda b, g, i: (g, b * nq + i, 0))],
        out_shape=[jax.ShapeDtypeStruct((n, N_HEADS * HEAD_DIM), F32),
                   jax.ShapeDtypeStruct((N_KV, n, nblk), BF16)],
        compiler_params=_params(("arbitrary", "arbitrary", "arbitrary")),
        name="cmp_topk")(q, cmpk, gates)


def _flash_kernel(q_ref, ks_ref, kw_ref, sel_ref, gate_ref, oc_ref, o_ref, acc_ref, *, n_back):
    tq = q_ref.shape[0]
    nblk = sel_ref.shape[1]
    tk = tq
    i = pl.program_id(2)
    t0 = i * tq
    rows = Q_PER_KV * tq
    qf = q_ref[...].astype(F32)
    lane = lax.broadcasted_iota(jnp.int32, (tq, LANES), 1)
    q4 = jnp.concatenate([_head_pad(qf, r, lane) for r in range(Q_PER_KV)], axis=0).astype(BF16)
    tpos = t0 + lax.broadcasted_iota(jnp.int32, (tq, tk), 0)
    col = lax.broadcasted_iota(jnp.int32, (tq, tk), 1)
    selm = sel_ref[...]
    e_row = lax.broadcasted_iota(jnp.int32, (nblk, tk), 0)
    e_blk = lax.shift_right_logical(lax.broadcasted_iota(jnp.int32, (nblk, tk), 1), 6)
    m0 = jnp.full((rows, 1), NEG, F32)
    l0 = jnp.zeros((rows, 1), F32)

    def step(kv_ref, kt, bias, carry):
        m, l = carry
        kv = kv_ref[pl.ds(pl.multiple_of(kt * tk, tk), tk), :]
        s = lax.dot_general(q4, kv, _NT, preferred_element_type=F32)
        s = s + jnp.concatenate([bias] * Q_PER_KV, axis=0)
        m_new = jnp.maximum(m, jnp.max(s, axis=-1, keepdims=True))
        alpha = jnp.exp(m - m_new)
        p = jnp.exp(s - m_new)
        l = alpha * l + jnp.sum(p, axis=-1, keepdims=True)
        acc_ref[...] = alpha * acc_ref[...] + _dot(p.astype(BF16), kv)
        return m_new, l

    def sel_body(kt, carry):
        expand = jnp.where(e_row == kt * (tk // SEL_BLOCK) + e_blk, 1.0, 0.0).astype(BF16)
        chosen = _dot(selm, expand)
        key = kt * tk + col
        bias = jnp.where(chosen > 0.5, jnp.where(key <= tpos, 0.0, NEG), NEG)
        return step(ks_ref, kt, bias, carry)

    acc_ref[...] = jnp.zeros_like(acc_ref)
    _, l_s = lax.fori_loop(0, i + 1, sel_body, (m0, l0))
    o_s = acc_ref[:, HEAD_DIM:] / l_s

    def win_bias(kt):
        dlt = tpos - (kt * tk + col)
        return jnp.where(dlt >= 0, jnp.where(dlt < WINDOW, 0.0, NEG), NEG)

    def win_body(j, carry):
        kt = i - 1 - j
        return step(kw_ref, kt, win_bias(kt), carry)

    acc_ref[...] = jnp.zeros_like(acc_ref)
    carry = step(kw_ref, i, win_bias(i), (m0, l0))
    _, l_w = lax.fori_loop(0, jnp.minimum(i, n_back), win_body, carry)
    o_w = acc_ref[:, HEAD_DIM:] / l_w

    gt = gate_ref[...]
    for r in range(Q_PER_KV):
        rs = slice(r * tq, (r + 1) * tq)
        o = (oc_ref[:, r * HEAD_DIM:(r + 1) * HEAD_DIM]
             + gt[:, 3 * r + 1:3 * r + 2] * o_s[rs] + gt[:, 3 * r + 2:3 * r + 3] * o_w[rs])
        o_ref[:, r * HEAD_DIM:(r + 1) * HEAD_DIM] = o.astype(BF16)


def _flash(q, kvp, sel, gates, oc, batch, seq, tq):
    n = q.shape[0]
    nblk = seq // SEL_BLOCK
    nq = seq // tq
    gw = Q_PER_KV * HEAD_DIM
    n_back = -(-(WINDOW - 1) // tq)
    row_g = lambda b, g, i: (b * nq + i, g)
    return pl.pallas_call(
        functools.partial(_flash_kernel, n_back=n_back),
        grid=(batch, N_KV, nq),
        in_specs=[pl.BlockSpec((tq, gw), row_g),
                  pl.BlockSpec((seq, LANES), lambda b, g, i: (b, N_KV + g)),
                  pl.BlockSpec((seq, LANES), lambda b, g, i: (b, 2 * N_KV + g)),
                  pl.BlockSpec((None, tq, nblk), lambda b, g, i: (g, b * nq + i, 0)),
                  pl.BlockSpec((tq, LANES), row_g),
                  pl.BlockSpec((tq, gw), row_g)],
        out_specs=pl.BlockSpec((tq, gw), row_g),
        out_shape=jax.ShapeDtypeStruct((n, N_HEADS * HEAD_DIM), BF16),
        scratch_shapes=[pltpu.VMEM((Q_PER_KV * tq, LANES), F32)],
        compiler_params=_params(("arbitrary", "arbitrary", "arbitrary")),
        name="flash_sel_win")(q, kvp, kvp, sel, gates, oc)


def _shift_rows(x, d, fill):
    row = lax.broadcasted_iota(jnp.int32, x.shape, 0)
    return jnp.where(row < d, fill, pltpu.roll(x, d, axis=0))


def _lagged(u, tail, lag):
    us = pltpu.roll(u, lag, axis=0)
    row = lax.broadcasted_iota(jnp.int32, tail.shape, 0)
    top = jnp.where(row < lag, pltpu.roll(tail, lag, axis=0), us[:SUBLANES])
    return jnp.concatenate([top, us[SUBLANES:]], axis=0)


def _mix_tail(x, h, onsa, v_sc, v_lru, wgm_ref, wsco_ref, wlo_ref, wno_ref, wo_ref):
    d = x.shape[1]
    u_sc = _dot(v_sc.astype(BF16), wsco_ref[...])
    u_lru = _dot(v_lru.astype(BF16), wlo_ref[...])
    u_nsa = _dot(onsa, wno_ref[...])
    mix = jax.nn.sigmoid(_dot(h, wgm_ref[:, :d])) * u_nsa
    mix = mix + jax.nn.sigmoid(_dot(h, wgm_ref[:, d:2 * d])) * u_sc
    mix = mix + jax.nn.sigmoid(_dot(h, wgm_ref[:, 2 * d:])) * u_lru
    return x + _dot(mix.astype(BF16), wo_ref[...])


def _lru_coeffs(xc, wgate_ref, bgate_ref, lam_ref):
    w = xc.shape[1]
    gates = jax.nn.sigmoid(_dot(xc.astype(BF16), wgate_ref[...]) + bgate_ref[...])
    lam = -lam_ref[...]
    softplus = jnp.maximum(lam, 0.0) + jnp.log1p(jnp.exp(-jnp.abs(lam)))
    log_a = -LRU_C * gates[:, :w] * softplus
    a = jnp.exp(log_a)
    th = jnp.tanh(log_a)
    mult = jnp.sqrt(-2.0 * th / (1.0 - th))
    return a, mult, gates[:, w:]


def _mix_prompt_kernel(x_ref, onsa_ref, g_ref, wb_ref, wgm_ref, wscc_ref, wlc_ref, blc_ref,
                       wgate_ref, bgate_ref, lam_ref, wsco_ref, wlo_ref, wno_ref, wo_ref,
                       x1_ref, sct_ref, lct_ref, hl_ref, sc_tail, lc_tail, h_prev):
    t = pl.program_id(1)
    tm = x_ref.shape[0]
    w = sc_tail.shape[1]

    @pl.when(t == 0)
    def _():
        sc_tail[...] = jnp.zeros_like(sc_tail)
        lc_tail[...] = jnp.zeros_like(lc_tail)
        h_prev[...] = jnp.zeros_like(h_prev)

    x = x_ref[...]
    h = _rms(x, g_ref[...]).astype(BF16)
    proj = _dot(h, wb_ref[...])
    b_sc, c_sc, x_sc = proj[:, :w], proj[:, w:2 * w], proj[:, 2 * w:3 * w]
    x_lru, y_lru = proj[:, 3 * w:4 * w], proj[:, 4 * w:]

    cx = c_sc * x_sc
    wscc = wscc_ref[...]
    tail = sc_tail[...]
    conv = cx * wscc[SC_K - 1:SC_K]
    for lag in range(1, SC_K):
        conv = conv + _lagged(cx, tail, lag) * wscc[SC_K - 1 - lag:SC_K - lag]
    sc_tail[...] = cx[tm - SUBLANES:]
    sct_ref[...] = cx[tm - SUBLANES:]
    v_sc = b_sc * conv

    wlc = wlc_ref[...]
    tail = lc_tail[...]
    xc = x_lru * wlc[LRU_K - 1:LRU_K]
    for lag in range(1, LRU_K):
        xc = xc + _lagged(x_lru, tail, lag) * wlc[LRU_K - 1 - lag:LRU_K - lag]
    xc = xc + blc_ref[...]
    lc_tail[...] = x_lru[tm - SUBLANES:]
    lct_ref[...] = x_lru[tm - SUBLANES:]

    a, mult, gate_i = _lru_coeffs(xc, wgate_ref, bgate_ref, lam_ref)
    reset = (t * tm + lax.broadcasted_iota(jnp.int32, a.shape, 0)) == 0
    a = jnp.where(reset, 0.0, a)
    mult = jnp.where(reset, 1.0, mult)
    bx = mult * gate_i * xc
    d = 1
    while d < tm:
        bx = a * _shift_rows(bx, d, 0.0) + bx
        a = a * _shift_rows(a, d, 1.0)
        d *= 2
    hs = a * h_prev[...] + bx
    h_prev[...] = hs[tm - 1:]
    hl_ref[...] = hs[tm - SUBLANES:]
    v_lru = hs * jax.nn.gelu(y_lru)

    x1_ref[...] = _mix_tail(x, h, onsa_ref[...], v_sc, v_lru, wgm_ref, wsco_ref, wlo_ref, wno_ref, wo_ref)


def _mix_decode_kernel(x_ref, onsa_ref, scp_ref, lcp_ref, h0_ref, g_ref, wb_ref, wgm_ref, wscc_ref,
                       wlc_ref, blc_ref, wgate_ref, bgate_ref, lam_ref, wsco_ref, wlo_ref, wno_ref,
                       wo_ref, x1_ref, cx_ref, xl_ref, hn_ref):
    w = h0_ref.shape[1]
    x = x_ref[...]
    h = _rms(x, g_ref[...]).astype(BF16)
    proj = _dot(h, wb_ref[...])
    b_sc, c_sc, x_sc = proj[:, :w], proj[:, w:2 * w], proj[:, 2 * w:3 * w]
    x_lru, y_lru = proj[:, 3 * w:4 * w], proj[:, 4 * w:]

    cx = c_sc * x_sc
    wscc = wscc_ref[...]
    conv = cx * wscc[SC_K - 1:SC_K]
    for k in range(SC_K - 1):
        conv = conv + scp_ref[:, k * w:(k + 1) * w] * wscc[k:k + 1]
    cx_ref[...] = cx
    v_sc = b_sc * conv

    wlc = wlc_ref[...]
    xc = x_lru * wlc[LRU_K - 1:LRU_K]
    for k in range(LRU_K - 1):
        xc = xc + lcp_ref[:, k * w:(k + 1) * w] * wlc[k:k + 1]
    xc = xc + blc_ref[...]
    xl_ref[...] = x_lru

    a, mult, gate_i = _lru_coeffs(xc, wgate_ref, bgate_ref, lam_ref)
    hs = a * h0_ref[...] + mult * gate_i * xc
    hn_ref[...] = hs
    v_lru = hs * jax.nn.gelu(y_lru)

    x1_ref[...] = _mix_tail(x, h, onsa_ref[...], v_sc, v_lru, wgm_ref, wsco_ref, wlo_ref, wno_ref, wo_ref)


def _mix_weight_args(lw):
    return [lw["g1"], lw["wb"], lw["wgm"], lw["wscc"], lw["wlc"], lw["blc"], lw["wgate"], lw["bgate"],
            lw["lam"], lw["wsco"], lw["wlo"], lw["wno"], lw["wo"]]


def _mix_prompt(x, onsa, lw, batch, seq, tm):
    n, d = x.shape
    w = lw["wsco"].shape[0]
    nt = seq // tm
    row = lambda b, t: (b * nt + t, 0)
    per_b = lambda b, t: (b, 0, 0)
    wargs = _mix_weight_args(lw)
    tail_shape = jax.ShapeDtypeStruct((batch, SUBLANES, w), F32)
    tail_spec = pl.BlockSpec((None, SUBLANES, w), per_b)
    return pl.pallas_call(
        _mix_prompt_kernel,
        grid=(batch, nt),
        in_specs=[pl.BlockSpec((tm, d), row), pl.BlockSpec((tm, onsa.shape[1]), row)]
        + [_const_spec(a.shape) for a in wargs],
        out_specs=[pl.BlockSpec((tm, d), row), tail_spec, tail_spec, tail_spec],
        out_shape=[jax.ShapeDtypeStruct((n, d), F32), tail_shape, tail_shape, tail_shape],
        scratch_shapes=[pltpu.VMEM((SUBLANES, w), F32), pltpu.VMEM((SUBLANES, w), F32),
                        pltpu.VMEM((1, w), F32)],
        compiler_params=_params(("arbitrary", "arbitrary")),
        name="mix_prompt")(x, onsa, *wargs)


def _mix_decode(x, onsa, sc_prev, lc_prev, h0, lw):
    n, d = x.shape
    w = h0.shape[1]
    wargs = _mix_weight_args(lw)
    full = lambda a: pl.BlockSpec(a.shape, lambda i: (0,) * a.ndim)
    acts = [x, onsa, sc_prev, lc_prev, h0]
    state = jax.ShapeDtypeStruct((n, w), F32)
    return pl.pallas_call(
        _mix_decode_kernel,
        grid=(1,),
        in_specs=[full(a) for a in acts] + [_const_spec(a.shape) for a in wargs],
        out_specs=[pl.BlockSpec((n, d), lambda i: (0, 0))] + [pl.BlockSpec((n, w), lambda i: (0, 0))] * 3,
        out_shape=[jax.ShapeDtypeStruct((n, d), F32), state, state, state],
        compiler_params=_params(("arbitrary",)),
        name="mix_decode")(*acts, *wargs)


def _ffn_kernel(*refs, final, n_chunks):
    if final:
        x_ref, g_ref, wi_ref, wout_ref, gf_ref, y_ref = refs
    else:
        x_ref, g_ref, wi_ref, wout_ref, y_ref = refs
    x = x_ref[...]
    h = _rms(x, g_ref[...]).astype(BF16)
    d_ff = wout_ref.shape[0]
    ck = d_ff // n_chunks
    y = x
    for c in range(n_chunks):
        gate = _dot(h, wi_ref[:, c * ck:(c + 1) * ck])
        up = _dot(h, wi_ref[:, d_ff + c * ck:d_ff + (c + 1) * ck])
        y = y + _dot((jax.nn.silu(gate) * up).astype(BF16), wout_ref[c * ck:(c + 1) * ck, :])
    if final:
        y = _rms(y, gf_ref[...])
    y_ref[...] = y


def _ffn(x, g, wi, wout, gf, tm):
    n, d = x.shape
    final = gf is not None
    d_ff = wout.shape[0]
    n_chunks = 2 if d_ff % (2 * LANES) == 0 else 1
    row = lambda i: (i, 0)
    args = [x, g, wi, wout] + ([gf] if final else [])
    return pl.pallas_call(
        functools.partial(_ffn_kernel, final=final, n_chunks=n_chunks),
        grid=(n // tm,),
        in_specs=[pl.BlockSpec((tm, d), row)] + [_const_spec(a.shape) for a in args[1:]],
        out_specs=pl.BlockSpec((tm, d), row),
        out_shape=jax.ShapeDtypeStruct((n, d), F32),
        compiler_params=_params(("arbitrary",)),
        name="ffn")(*args)


def _group_onehot(rows, cols, row_div, col_div):
    r = lax.broadcasted_iota(jnp.int32, (rows, cols), 0) // row_div
    c = lax.broadcasted_iota(jnp.int32, (rows, cols), 1) // col_div
    return r == c


def _tile_lanes(q, reps):
    hd = q.shape[1]
    r = lax.broadcasted_iota(jnp.int32, (hd, reps * hd), 0)
    c = lax.broadcasted_iota(jnp.int32, (hd, reps * hd), 1)
    rep = jnp.where(c % hd == r, 1.0, 0.0).astype(BF16)
    return _dot(q, rep)


def _fold_heads(o):
    own = _group_onehot(o.shape[0], o.shape[1], Q_PER_KV, HEAD_DIM)
    o = jnp.where(own, o, 0.0)
    out = o[:, :HEAD_DIM]
    for g in range(1, o.shape[1] // HEAD_DIM):
        out = out + o[:, g * HEAD_DIM:(g + 1) * HEAD_DIM]
    return out


def _dec_cmp_kernel(pt_ref, q_ref, gate_ref, wc_ref, cache_ref, oc_ref, idx_ref, buf, sem,
                    *, n_pages, page, past, n_sel):
    b = pl.program_id(0)
    nb = pl.num_programs(0)
    width = buf.shape[2]

    def copies(bb, slot):
        return [pltpu.make_async_copy(cache_ref.at[pt_ref[bb * n_pages + pg], :, pl.ds(0, width)],
                                      buf.at[slot, pl.ds(pg * page, page), :], sem.at[slot])
                for pg in range(n_pages)]

    slot = b % 2

    @pl.when(b == 0)
    def _():
        for c in copies(0, 0):
            c.start()

    @pl.when(b + 1 < nb)
    def _():
        for c in copies(b + 1, 1 - slot):
            c.start()

    for c in copies(b, slot):
        c.wait()

    nblk = past // CMP_BLOCK
    npad = -(-(nblk + 1) // LANES) * LANES
    kv = buf[slot].reshape(nblk, CMP_BLOCK, width)
    kvc = jnp.sum(kv * wc_ref[...][None], axis=1)
    kvc = jnp.concatenate([kvc, jnp.zeros((npad - nblk, width), F32)], axis=0).astype(BF16)
    gw = N_KV * HEAD_DIM
    kc, vc = kvc[:, :gw], kvc[:, gw:]

    q = q_ref[...]
    own = _group_onehot(N_HEADS, gw, Q_PER_KV, HEAD_DIM)
    qbd = jnp.where(own, _tile_lanes(q, N_KV), 0.0).astype(BF16)
    s = lax.dot_general(qbd, kc, _NT, preferred_element_type=F32)
    blk = lax.broadcasted_iota(jnp.int32, (N_HEADS, npad), 1)
    valid = (blk + 1) * CMP_BLOCK - 1 <= past
    s = jnp.where(valid, s, NEG)
    m = jnp.max(s, axis=-1, keepdims=True)
    p = jnp.where(valid, jnp.exp(s - m), 0.0)
    p = p / jnp.maximum(jnp.sum(p, axis=-1, keepdims=True), TINY)
    o = _fold_heads(_dot(p.astype(BF16), vc))
    oc_ref[...] = gate_ref[:, 0:1] * o

    blk8 = lax.broadcasted_iota(jnp.int32, (SUBLANES, npad), 1)
    row8 = lax.broadcasted_iota(jnp.int32, (SUBLANES, npad), 0)
    imp = jnp.zeros((SUBLANES, npad), F32)
    for g in range(N_KV):
        psum = jnp.sum(p[g * Q_PER_KV:(g + 1) * Q_PER_KV], axis=0, keepdims=True)
        imp = jnp.where(row8 == g, psum, imp)
    cur = past // SEL_BLOCK
    imp = imp + jnp.where(blk8 == cur, FORCE, 0.0) + jnp.where(blk8 == 0, 0.5 * FORCE, 0.0)
    imp = jnp.where(blk8 <= cur, imp, -jnp.inf)
    blkf = blk8.astype(F32)
    idx = jnp.zeros((SUBLANES, LANES), F32)
    lane = lax.broadcasted_iota(jnp.int32, (SUBLANES, LANES), 1)
    for j in range(n_sel):
        mx = jnp.max(imp, axis=-1, keepdims=True)
        first = jnp.min(jnp.where(imp == mx, blkf, float(npad)), axis=-1, keepdims=True)
        idx = jnp.where(lane == j, first, idx)
        imp = jnp.where(blkf == first, -jnp.inf, imp)
    idx_ref[...] = idx.astype(jnp.int32)


def _dec_cmp(page_table, q3, gates3, wc, cache, past):
    db = q3.shape[0]
    n_pages = page_table.shape[1]
    page = cache.shape[1]
    width = wc.shape[1]
    n_sel = min(N_SEL, past // SEL_BLOCK + 1)
    grid_spec = pltpu.PrefetchScalarGridSpec(
        num_scalar_prefetch=1,
        grid=(db,),
        in_specs=[pl.BlockSpec((None, N_HEADS, HEAD_DIM), lambda b, pt: (b, 0, 0)),
                  pl.BlockSpec((None, N_HEADS, LANES), lambda b, pt: (b, 0, 0)),
                  pl.BlockSpec(wc.shape, lambda b, pt: (0, 0)),
                  pl.BlockSpec(memory_space=pl.ANY)],
        out_specs=[pl.BlockSpec((None, N_HEADS, HEAD_DIM), lambda b, pt: (b, 0, 0)),
                   pl.BlockSpec((None, SUBLANES, LANES), lambda b, pt: (b, 0, 0))],
        scratch_shapes=[pltpu.VMEM((2, past, width), F32), pltpu.SemaphoreType.DMA((2,))])
    return pl.pallas_call(
        functools.partial(_dec_cmp_kernel, n_pages=n_pages, page=page, past=past, n_sel=n_sel),
        grid_spec=grid_spec,
        out_shape=[jax.ShapeDtypeStruct((db, N_HEADS, HEAD_DIM), F32),
                   jax.ShapeDtypeStruct((db, SUBLANES, LANES), jnp.int32)],
        compiler_params=_params(("arbitrary",)),
        name="dec_cmp_topk")(page_table.reshape(-1), q3, gates3, wc, cache)


def _dec_attn_kernel(idx_ref, pt_ref, q_ref, gate_ref, oc_ref, kvn_ref, kwn_ref, win_ref, cache_ref,
                     o_ref, kbuf, vbuf, sem, *, n_sel, past, page, win_len):
    b = pl.program_id(0)
    nblk_past = past // SEL_BLOCK
    gw = N_KV * HEAD_DIM
    per_page = page // SEL_BLOCK

    def block_copies(g, j, n):
        pg = pt_ref[b * (nblk_past // per_page) + n // per_page]
        r0 = (n % per_page) * SEL_BLOCK
        c0 = (g // 2) * LANES
        dst = pl.ds(j * SEL_BLOCK, SEL_BLOCK)
        return (pltpu.make_async_copy(cache_ref.at[pg, pl.ds(r0, SEL_BLOCK), pl.ds(2 * gw + c0, LANES)],
                                      kbuf.at[g, dst, :], sem.at[0]),
                pltpu.make_async_copy(cache_ref.at[pg, pl.ds(r0, SEL_BLOCK), pl.ds(3 * gw + c0, LANES)],
                                      vbuf.at[g, dst, :], sem.at[1]))

    for g in range(N_KV):
        for j in range(n_sel):
            n = idx_ref[(b * N_KV + g) * n_sel + j]

            @pl.when(n < nblk_past)
            def _():
                for c in block_copies(g, j, n):
                    c.start()

            @pl.when(n >= nblk_past)
            def _():
                zeros = jnp.zeros((SEL_BLOCK, LANES), F32)
                kbuf[g, j * SEL_BLOCK:(j + 1) * SEL_BLOCK, :] = zeros
                vbuf[g, j * SEL_BLOCK:(j + 1) * SEL_BLOCK, :] = zeros

    q = q_ref[...]
    own = _group_onehot(N_HEADS, gw, Q_PER_KV, HEAD_DIM)
    qbd_f = jnp.where(own, _tile_lanes(q, N_KV), 0.0)
    qbd = qbd_f.astype(BF16)
    gt = gate_ref[...]

    win = win_ref[...]
    kw, vw = win[:, :gw].astype(BF16), win[:, gw:].astype(BF16)
    s = lax.dot_general(qbd, kw, _NT, preferred_element_type=F32)
    colw = lax.broadcasted_iota(jnp.int32, s.shape, 1)
    s = jnp.where(win_len - colw < WINDOW, s, NEG)
    kwn = kwn_ref[...]
    s_new = jnp.sum(qbd_f * kwn[:, :gw], axis=-1, keepdims=True)
    m = jnp.maximum(jnp.max(s, axis=-1, keepdims=True), s_new)
    p = jnp.exp(s - m)
    p_new = jnp.exp(s_new - m)
    l = jnp.sum(p, axis=-1, keepdims=True) + p_new
    o_w = _fold_heads(_dot(p.astype(BF16), vw) + p_new * kwn[:, gw:]) / l

    for g in range(N_KV):
        for j in range(n_sel):
            n = idx_ref[(b * N_KV + g) * n_sel + j]

            @pl.when(n < nblk_past)
            def _():
                for c in block_copies(g, j, n):
                    c.wait()

    nkeys = n_sel * SEL_BLOCK
    rowg = lax.broadcasted_iota(jnp.int32, (N_HEADS, LANES), 0) // Q_PER_KV
    laneh = lax.broadcasted_iota(jnp.int32, (N_HEADS, LANES), 1) // HEAD_DIM
    half = (rowg % 2) == laneh
    q2 = jnp.where(half, _tile_lanes(q, 2), 0.0)
    rows = lax.broadcasted_iota(jnp.int32, (N_HEADS, nkeys), 0) // Q_PER_KV
    colj = lax.broadcasted_iota(jnp.int32, (N_HEADS, nkeys), 1) // SEL_BLOCK
    s = jnp.zeros((N_HEADS, nkeys), F32)
    okay = jnp.zeros((N_HEADS, nkeys), F32)
    for g in range(N_KV):
        qg = jnp.where(rowg == g, q2, 0.0).astype(BF16)
        s = s + lax.dot_general(qg, kbuf[g].astype(BF16), _NT, preferred_element_type=F32)
        for j in range(n_sel):
            n = idx_ref[(b * N_KV + g) * n_sel + j]
            flag = jnp.where(n < nblk_past, 1.0, 0.0)
            okay = jnp.where(rows == g, jnp.where(colj == j, flag, okay), okay)
    s = jnp.where(okay > 0.5, s, NEG)
    kvn = kvn_ref[...]
    s_new = jnp.sum(qbd_f * kvn[:, 2 * gw:3 * gw], axis=-1, keepdims=True)
    m = jnp.maximum(jnp.max(s, axis=-1, keepdims=True), s_new)
    p = jnp.where(okay > 0.5, jnp.exp(s - m), 0.0)
    p_new = jnp.exp(s_new - m)
    l = jnp.sum(p, axis=-1, keepdims=True) + p_new
    o2 = jnp.zeros((N_HEADS, LANES), F32)
    for g in range(N_KV):
        pg = jnp.where(rows == g, p, 0.0).astype(BF16)
        o2 = o2 + _dot(pg, vbuf[g].astype(BF16))
    o2 = jnp.where(half, o2, 0.0)
    o_sel = o2[:, :HEAD_DIM] + o2[:, HEAD_DIM:]
    o_s = (o_sel + _fold_heads(p_new * jnp.where(own, kvn[:, 3 * gw:], 0.0))) / l

    o_ref[...] = (oc_ref[...] + gt[:, 1:2] * o_s + gt[:, 2:3] * o_w).astype(BF16)


def _dec_attn(idx, page_table, q3, gates3, oc, kv_new, kw_new, win, cache, past):
    db = q3.shape[0]
    page = cache.shape[1]
    win_len = win.shape[1]
    n_sel = idx.shape[-1]
    per_b = lambda b, *_: (b, 0, 0)
    grid_spec = pltpu.PrefetchScalarGridSpec(
        num_scalar_prefetch=2,
        grid=(db,),
        in_specs=[pl.BlockSpec((None, N_HEADS, HEAD_DIM), per_b),
                  pl.BlockSpec((None, N_HEADS, LANES), per_b),
                  pl.BlockSpec((None, N_HEADS, HEAD_DIM), per_b),
                  pl.BlockSpec((None, 1, kv_new.shape[-1]), per_b),
                  pl.BlockSpec((None, 1, kw_new.shape[-1]), per_b),
                  pl.BlockSpec((None, win_len, win.shape[2]), per_b),
                  pl.BlockSpec(memory_space=pl.ANY)],
        out_specs=pl.BlockSpec((None, N_HEADS, HEAD_DIM), per_b),
        scratch_shapes=[pltpu.VMEM((N_KV, n_sel * SEL_BLOCK, LANES), F32),
                        pltpu.VMEM((N_KV, n_sel * SEL_BLOCK, LANES), F32),
                        pltpu.SemaphoreType.DMA((2,))])
    return pl.pallas_call(
        functools.partial(_dec_attn_kernel, n_sel=n_sel, past=past, page=page, win_len=win_len),
        grid_spec=grid_spec,
        out_shape=jax.ShapeDtypeStruct((db, N_HEADS, HEAD_DIM), BF16),
        compiler_params=_params(("arbitrary",)),
        name="dec_attn")(idx.reshape(-1), page_table.reshape(-1), q3, gates3, oc,
                         kv_new[:, None, :], kw_new[:, None, :], win, cache)


def _layer_weights(l, norm1_g, w_in, w_cmp_k, w_cmp_v, w_nsa_out, w_sc_conv, w_sc_out, w_lru_conv,
                   b_lru_conv, w_lru_gate, b_lru_gate, lru_lambda, w_lru_out, w_o, norm2_g, w_ffn_in,
                   w_ffn_out):
    d = w_in.shape[1]
    nsa = N_HEADS * HEAD_DIM
    kvc = 2 * N_KV * HEAD_DIM
    sc_w = w_sc_out.shape[1]
    lru_w = w_lru_out.shape[1]
    wi = w_in[l]
    o = 0
    wq = wi[:, o:o + nsa]; o += nsa
    wkv = wi[:, o:o + 3 * kvc]; o += 3 * kvc
    wg = wi[:, o:o + 3 * N_HEADS]; o += 3 * N_HEADS
    wb = wi[:, o:o + 3 * sc_w + 2 * lru_w]; o += 3 * sc_w + 2 * lru_w
    wgm = wi[:, o:]
    wkp = wkv.reshape(d, 3, 2, N_KV, HEAD_DIM).transpose(0, 1, 3, 2, 4).reshape(d, 3 * kvc)
    pad_g = ((0, 0), (0, 0), (0, LANES - 3 * Q_PER_KV))
    wg_p = jnp.pad(wg.reshape(d, N_KV, 3 * Q_PER_KV), pad_g).reshape(d, N_KV * LANES)
    pad_h = ((0, 0), (0, 0), (0, LANES - 3))
    wg_s = jnp.pad(wg.reshape(d, N_HEADS, 3), pad_h).reshape(d, N_HEADS * LANES)
    ck, cv = w_cmp_k[l], w_cmp_v[l]
    wc_p = jnp.tile(jnp.concatenate([ck, cv], axis=1), (1, N_KV))
    wc_s = jnp.concatenate([jnp.tile(ck, (1, N_KV)), jnp.tile(cv, (1, N_KV))], axis=1)
    bw = lru_w // LRU_HEADS
    eye = jnp.eye(LRU_HEADS, dtype=F32)
    wgate = jnp.concatenate(
        [(eye[:, None, :, None] * w_lru_gate[l, k][:, :, None, :]).reshape(lru_w, lru_w) for k in range(2)],
        axis=1)
    bf = lambda a: a.astype(BF16)
    return {
        "g1": norm1_g[l][None], "wq": bf(wq), "wkp": bf(wkp), "wkv": bf(wkv),
        "wg_p": bf(wg_p), "wg_s": bf(wg_s), "wc_p": wc_p, "wc_s": wc_s,
        "wb": bf(wb), "wgm": bf(wgm), "wscc": w_sc_conv[l], "wlc": w_lru_conv[l],
        "blc": b_lru_conv[l][None], "wgate": bf(wgate), "bgate": b_lru_gate[l].reshape(1, 2 * lru_w),
        "lam": lru_lambda[l][None], "wsco": bf(w_sc_out[l]), "wlo": bf(w_lru_out[l]),
        "wno": bf(w_nsa_out[l]), "wo": bf(w_o[l]), "g2": norm2_g[l][None],
        "wfi": bf(w_ffn_in[l]), "wfo": bf(w_ffn_out[l]),
    }


def _pick_tile(n, pref):
    t = min(n, pref)
    while n % t:
        t //= 2
    return t


def kernel(x_prompt, x_sample, cache_kv, cache_win, state_sconv, state_lru_conv, state_lru_h, page_table, norm1_g, w_in, w_cmp_k, w_cmp_v, w_nsa_out, w_sc_conv, w_sc_out, w_lru_conv, b_lru_conv, w_lru_gate, b_lru_gate, lru_lambda, w_lru_out, w_o, norm2_g, w_ffn_in, w_ffn_out, final_g):
    batch, seq, d = x_prompt.shape
    db, dec_seq, _ = x_sample.shape
    depth = w_in.shape[0]
    page = cache_kv.shape[2]
    past = page_table.shape[1] * page
    win_len = cache_win.shape[2]
    row_w = 4 * N_KV * HEAD_DIM
    assert dec_seq == 1 and past % SEL_BLOCK == 0 and page % SEL_BLOCK == 0
    assert seq % 256 == 0

    tq = 256
    tm_qkv = _pick_tile(batch * seq, 512)
    tm_mix = _pick_tile(seq, 256)
    tm_ffn = _pick_tile(batch * seq, 512)
    gf = final_g[None]

    xp = x_prompt.reshape(batch * seq, d)
    xs = x_sample.reshape(db, d)
    outs_p, outs_s = [], []
    for l in range(depth):
        lw = _layer_weights(l, norm1_g, w_in, w_cmp_k, w_cmp_v, w_nsa_out, w_sc_conv, w_sc_out,
                            w_lru_conv, b_lru_conv, w_lru_gate, b_lru_gate, lru_lambda, w_lru_out,
                            w_o, norm2_g, w_ffn_in, w_ffn_out)
        last = l == depth - 1
        q, kvp, kvr, kvw, gates, cmpk = _qkv_proj(xp, lw["g1"], lw["wq"], lw["wkp"], lw["wkv"],
                                                  lw["wg_p"], lw["wc_p"], tm_qkv)
        oc, sel = _cmp_topk(q, cmpk, gates, batch, seq, tq)
        onsa = _flash(q, kvp, sel, gates, oc, batch, seq, tq)
        x1, sct, lct, hl = _mix_prompt(xp, onsa, lw, batch, seq, tm_mix)
        xp = _ffn(x1, lw["g2"], lw["wfi"], lw["wfo"], gf if last else None, tm_ffn)
        w_keep = min(WINDOW, seq)
        outs_p.append((
            kvr.reshape(batch, seq, 4, N_KV, HEAD_DIM),
            kvw.reshape(batch, seq, 2, N_KV, HEAD_DIM)[:, seq - w_keep:],
            sct[:, SUBLANES - (SC_K - 1):],
            lct[:, SUBLANES - (LRU_K - 1):],
            hl[:, SUBLANES - 1],
        ))
        cache_l = cache_kv[l].reshape(cache_kv.shape[1], page, row_w)
        win_l = cache_win[l].reshape(db, win_len, 2 * N_KV * HEAD_DIM)
        q, kvp, kvr, kvw, gates = _qkv_proj(xs, lw["g1"], lw["wq"], lw["wkp"], lw["wkv"],
                                            lw["wg_s"], None, db)
        q3 = q.reshape(db, N_HEADS, HEAD_DIM)
        gates3 = gates.reshape(db, N_HEADS, LANES)
        oc, idx = _dec_cmp(page_table, q3, gates3, lw["wc_s"], cache_l, past)
        n_sel = min(N_SEL, past // SEL_BLOCK + 1)
        idx = idx[:, :N_KV, :n_sel]
        onsa = _dec_attn(idx, page_table, q3, gates3, oc, kvr, kvw, win_l, cache_l, past)
        sc_prev = state_sconv[l].reshape(db, -1)
        lc_prev = state_lru_conv[l].reshape(db, -1)
        x1, cx, xl, hn = _mix_decode(xs, onsa.reshape(db, N_HEADS * HEAD_DIM), sc_prev, lc_prev,
                                     state_lru_h[l], lw)
        xs = _ffn(x1, lw["g2"], lw["wfi"], lw["wfo"], gf if last else None, db)
        outs_s.append((
            kvr.reshape(db, 1, 4, N_KV, HEAD_DIM),
            jnp.concatenate([cache_win[l][:, 1:], kvw.reshape(db, 1, 2, N_KV, HEAD_DIM)], axis=1),
            jnp.concatenate([state_sconv[l][:, 1:], cx[:, None]], axis=1),
            jnp.concatenate([state_lru_conv[l][:, 1:], xl[:, None]], axis=1),
            hn,
        ))
    stack = lambda outs: tuple(jnp.stack([o[i] for o in outs]) for i in range(5))
    y_prompt = xp.reshape(batch, seq, d)
    y_sample = xs.reshape(db, 1, d)
    return (y_prompt, y_sample) + stack(outs_p) + stack(outs_s)
```

```python
import functools

import jax
import jax.numpy as jnp
from jax import lax
from jax.experimental import pallas as pl
from jax.experimental.pallas import tpu as pltpu

N_HEADS = 16
N_KV = 4
HEAD_DIM = 64
Q_PER_KV = N_HEADS // N_KV
CMP_BLOCK = 64
SEL_BLOCK = CMP_BLOCK
N_SEL = 8
WINDOW = 512
SC_K = 3
LRU_K = 4
LRU_HEADS = 8
LRU_C = 8.0
SCALE = HEAD_DIM ** -0.5
EPS = 1e-6
NEG = -1e30
TINY = 1e-20
FORCE = 1e6

LANES = 128
SUBLANES = 8
VMEM_LIMIT = 56 * 1024 * 1024
GROUP_W = N_KV * HEAD_DIM

F32 = jnp.float32
BF16 = jnp.bfloat16
_NT = (((1,), (1,)), ((), ()))


def _params(sem):
    return pltpu.CompilerParams(dimension_semantics=sem, vmem_limit_bytes=VMEM_LIMIT)


def _const_spec(shape):
    nd = len(shape)
    return pl.BlockSpec(shape, lambda *_: (0,) * nd, pipeline_mode=pl.Buffered(1))


def _rms(x, g):
    y = x * lax.rsqrt(jnp.mean(x * x, axis=-1, keepdims=True) + EPS)
    return y * g


def _dot(a, b):
    return jnp.dot(a, b, preferred_element_type=F32)


def _dot_nt(a, b):
    return lax.dot_general(a, b, _NT, preferred_element_type=F32)


def _qkv_kernel(*refs, prompt, tk):
    if prompt:
        (x_ref, g_ref, wq_ref, wkvt_ref, wg_ref, wkc_ref, wc_ref,
         q_ref, kvt_ref, kwt_ref, gate_ref, kvb_ref, cmp_ref) = refs
    else:
        (x_ref, g_ref, wq_ref, wkvt_ref, wg_ref, wkv_ref,
         q_ref, kvt_ref, kwt_ref, gate_ref, kvr_ref) = refs
    h = _rms(x_ref[...], g_ref[...]).astype(BF16)
    q_ref[...] = (_dot(h, wq_ref[...]) * SCALE).astype(BF16)
    kvt = _dot_nt(wkvt_ref[...], h)
    n_rows = kvt_ref.shape[0]
    kvt_ref[...] = kvt[:n_rows]
    kwt_ref[...] = kvt[n_rows:]
    gate_ref[...] = jax.nn.sigmoid(_dot(h, wg_ref[...]))
    if prompt:
        tm = x_ref.shape[0]
        skip = n_rows // 2
        for c in range(tm // tk):
            kvb_ref[c] = kvt[skip:, c * tk:(c + 1) * tk].astype(BF16)
        width = wc_ref.shape[1]
        blocks = _dot(h, wkc_ref[...]).reshape(tm // CMP_BLOCK, CMP_BLOCK, width)
        cmp_ref[...] = jnp.sum(blocks * wc_ref[...][None], axis=1)
    else:
        kvr_ref[...] = _dot(h, wkv_ref[...])


def _qkv_proj(x, g, wq, wkvt, wg, extra, batch, seq, tm, tk=None):
    n, d = x.shape
    prompt = tk is not None
    nt = seq // tm
    n_rows = 4 * GROUP_W
    n_win = wkvt.shape[0] - n_rows
    row = lambda b, t: (b * nt + t, 0)
    colblk = lambda b, t: (b, 0, t)
    args = [x, g, wq, wkvt, wg, *extra]
    in_specs = [pl.BlockSpec((tm, d), row)] + [_const_spec(a.shape) for a in args[1:]]
    out_shape = [jax.ShapeDtypeStruct((n, wq.shape[1]), BF16),
                 jax.ShapeDtypeStruct((batch, n_rows, seq), F32),
                 jax.ShapeDtypeStruct((batch, n_win, seq), F32),
                 jax.ShapeDtypeStruct((n, wg.shape[1]), F32)]
    out_specs = [pl.BlockSpec((tm, wq.shape[1]), row),
                 pl.BlockSpec((None, n_rows, tm), colblk),
                 pl.BlockSpec((None, n_win, tm), colblk),
                 pl.BlockSpec((tm, wg.shape[1]), row)]
    if prompt:
        wc = extra[1]
        nb = n_rows // 2 + n_win
        out_shape += [jax.ShapeDtypeStruct((batch, seq // tk, nb, tk), BF16),
                      jax.ShapeDtypeStruct((n // CMP_BLOCK, wc.shape[1]), F32)]
        out_specs += [pl.BlockSpec((None, tm // tk, nb, tk), lambda b, t: (b, t, 0, 0)),
                      pl.BlockSpec((tm // CMP_BLOCK, wc.shape[1]), row)]
    else:
        wkv = extra[0]
        out_shape.append(jax.ShapeDtypeStruct((n, wkv.shape[1]), F32))
        out_specs.append(pl.BlockSpec((tm, wkv.shape[1]), row))
    return pl.pallas_call(
        functools.partial(_qkv_kernel, prompt=prompt, tk=tk),
        grid=(batch, nt), in_specs=in_specs, out_specs=out_specs, out_shape=out_shape,
        compiler_params=_params(("arbitrary", "arbitrary")), name="qkv_proj")(*args)


def _head_pad(qf, r, lane):
    pair = qf[:, (r // 2) * LANES:(r // 2 + 1) * LANES]
    if r % 2 == 1:
        pair = pltpu.roll(pair, HEAD_DIM, axis=1)
    return jnp.where(lane < HEAD_DIM, pair, 0.0)


def _cmp_topk_kernel(q_ref, cmp_ref, gate_ref, oc_ref, sel_ref, *, n_sel):
    tq = q_ref.shape[0]
    nblk = cmp_ref.shape[0]
    t0 = pl.program_id(2) * tq
    kcb = cmp_ref[...].astype(BF16)
    qf = q_ref[...].astype(F32)
    lane = lax.broadcasted_iota(jnp.int32, (tq, LANES), 1)
    tpos = t0 + lax.broadcasted_iota(jnp.int32, (tq, nblk), 0)
    blk = lax.broadcasted_iota(jnp.int32, (tq, nblk), 1)
    valid = (blk + 1) * CMP_BLOCK - 1 <= tpos
    gt = gate_ref[...]
    imp = jnp.zeros((tq, nblk), F32)
    for r in range(Q_PER_KV):
        qp = _head_pad(qf, r, lane).astype(BF16)
        s = _dot_nt(qp, kcb)
        s = jnp.where(valid, s, NEG)
        m = jnp.max(s, axis=-1, keepdims=True)
        p = jnp.where(valid, jnp.exp(s - m), 0.0)
        p = p / jnp.maximum(jnp.sum(p, axis=-1, keepdims=True), TINY)
        imp = imp + p
        o = _dot(p.astype(BF16), kcb)
        oc_ref[:, r * HEAD_DIM:(r + 1) * HEAD_DIM] = gt[:, 3 * r:3 * r + 1] * o[:, HEAD_DIM:]
    cur = lax.shift_right_logical(tpos, 6)
    imp = imp + jnp.where(blk == cur, FORCE, 0.0) + jnp.where(blk == 0, 0.5 * FORCE, 0.0)
    imp = jnp.where(blk <= cur, imp, -FORCE)
    blkf = blk.astype(F32)
    sel = jnp.zeros((tq, nblk), F32)
    for _ in range(n_sel):
        mx = jnp.max(imp, axis=-1, keepdims=True)
        first = jnp.min(jnp.where(imp == mx, blkf, float(nblk)), axis=-1, keepdims=True)
        hit = blkf == first
        sel = jnp.where(hit, 1.0, sel)
        imp = jnp.where(hit, -jnp.inf, imp)
    sel_ref[...] = jnp.where(blk <= cur, sel, 0.0).astype(BF16)


def _cmp_topk(q, cmpk, gates, batch, seq, tq):
    n = q.shape[0]
    nblk = seq // CMP_BLOCK
    nq = seq // tq
    gw = Q_PER_KV * HEAD_DIM
    row_g = lambda b, g, i: (b * nq + i, g)
    return pl.pallas_call(
        functools.partial(_cmp_topk_kernel, n_sel=min(N_SEL, nblk)),
        grid=(batch, N_KV, nq),
        in_specs=[pl.BlockSpec((tq, gw), row_g),
                  pl.BlockSpec((nblk, LANES), lambda b, g, i: (b, g)),
                  pl.BlockSpec((tq, LANES), row_g)],
        out_specs=[pl.BlockSpec((tq, gw), row_g),
                   pl.BlockSpec((None, tq, nblk), lambda b, g, i: (g, b * nq + i, 0))],
        out_shape=[jax.ShapeDtypeStruct((n, N_HEADS * HEAD_DIM), F32),
                   jax.ShapeDtypeStruct((N_KV, n, nblk), BF16)],
        compiler_params=_params(("arbitrary", "arbitrary", "arbitrary")),
        name="cmp_topk")(q, cmpk, gates)


def _flash_kernel(q_ref, ks_ref, vs_ref, kw_ref, vw_ref, sel_ref, gate_ref, oc_ref, o_ref,
                  q4_ref, bias_ref, m_ref, l_ref, acc_ref, out_ref, *, n_back, rc):
    tq = q_ref.shape[0]
    nblk = sel_ref.shape[1]
    tk = ks_ref.shape[2]
    i = pl.program_id(2)
    t0 = i * tq
    rows = Q_PER_KV * tq
    qf = q_ref[...].astype(F32)
    for r in range(Q_PER_KV):
        q4_ref[r * tq:(r + 1) * tq, :] = qf[:, r * HEAD_DIM:(r + 1) * HEAD_DIM].astype(BF16)
    tpos = t0 + lax.broadcasted_iota(jnp.int32, (tq, tk), 0)
    col = lax.broadcasted_iota(jnp.int32, (tq, tk), 1)
    e_row = lax.broadcasted_iota(jnp.int32, (nblk, tk), 0)
    e_blk = lax.shift_right_logical(lax.broadcasted_iota(jnp.int32, (nblk, tk), 1), 6)

    def reset():
        m_ref[...] = jnp.full(m_ref.shape, NEG, F32)
        l_ref[...] = jnp.zeros(l_ref.shape, F32)
        acc_ref[...] = jnp.zeros(acc_ref.shape, F32)

    def attend(k_ref, v_ref, kt):
        kt_k = k_ref[kt]
        kt_v = v_ref[kt]
        for c in range(rows // rc):
            rs = slice(c * rc, (c + 1) * rc)
            b0 = (c * rc) % tq
            s = _dot(q4_ref[rs, :], kt_k) + bias_ref[b0:b0 + rc, :]
            m_old = m_ref[rs, :]
            m_new = jnp.maximum(m_old, jnp.max(s, axis=-1, keepdims=True))
            alpha = jnp.exp(m_old - m_new)
            p = jnp.exp(s - jnp.concatenate([m_new] * (tk // LANES), axis=1))
            l_ref[rs, :] = alpha * l_ref[rs, :] + jnp.sum(p, axis=-1, keepdims=True)
            acc_ref[rs, :] = alpha[:, :HEAD_DIM] * acc_ref[rs, :] + _dot_nt(p.astype(BF16), kt_v)
            m_ref[rs, :] = m_new

    def emit(k):
        gt = gate_ref[...]
        for r in range(Q_PER_KV):
            rs = slice(r * tq, (r + 1) * tq)
            hs = slice(r * HEAD_DIM, (r + 1) * HEAD_DIM)
            scale = gt[:, 3 * r + k:3 * r + k + 1] / l_ref[rs, :HEAD_DIM]
            out_ref[:, hs] = out_ref[:, hs] + scale * acc_ref[rs, :]

    out_ref[...] = oc_ref[...]

    def sel_body(kt, carry):
        expand = jnp.where(e_row == kt * (tk // SEL_BLOCK) + e_blk, 1.0, 0.0).astype(BF16)
        chosen = _dot(sel_ref[...], expand)
        key = kt * tk + col
        bias_ref[...] = jnp.where(chosen > 0.5, jnp.where(key <= tpos, 0.0, NEG), NEG)
        attend(ks_ref, vs_ref, kt)
        return carry

    reset()
    lax.fori_loop(0, i + 1, sel_body, 0)
    emit(1)

    def win_step(kt):
        dlt = tpos - (kt * tk + col)
        bias_ref[...] = jnp.where(dlt >= 0, jnp.where(dlt < WINDOW, 0.0, NEG), NEG)
        attend(kw_ref, vw_ref, kt)

    def win_body(j, carry):
        win_step(i - 1 - j)
        return carry

    reset()
    win_step(i)
    lax.fori_loop(0, jnp.minimum(i, n_back), win_body, 0)
    emit(2)
    o_ref[...] = out_ref[...].astype(BF16)


def _flash(q, kvb, sel, gates, oc, batch, seq, tq, rc):
    n = q.shape[0]
    nblk = seq // SEL_BLOCK
    nq = seq // tq
    nkt, tk = kvb.shape[1], kvb.shape[3]
    assert tk == tq
    gw = Q_PER_KV * HEAD_DIM
    n_back = -(-(WINDOW - 1) // tk)
    row_g = lambda b, g, i: (b * nq + i, g)
    kv_spec = lambda kind: pl.BlockSpec((None, nkt, HEAD_DIM, tk),
                                        lambda b, g, i: (b, 0, kind * N_KV + g, 0))
    rows = Q_PER_KV * tq
    return pl.pallas_call(
        functools.partial(_flash_kernel, n_back=n_back, rc=rc),
        grid=(batch, N_KV, nq),
        in_specs=[pl.BlockSpec((tq, gw), row_g), kv_spec(0), kv_spec(1), kv_spec(2), kv_spec(3),
                  pl.BlockSpec((None, tq, nblk), lambda b, g, i: (g, b * nq + i, 0)),
                  pl.BlockSpec((tq, LANES), row_g),
                  pl.BlockSpec((tq, gw), row_g)],
        out_specs=pl.BlockSpec((tq, gw), row_g),
        out_shape=jax.ShapeDtypeStruct((n, N_HEADS * HEAD_DIM), BF16),
        scratch_shapes=[pltpu.VMEM((rows, HEAD_DIM), BF16), pltpu.VMEM((tq, tk), F32),
                        pltpu.VMEM((rows, LANES), F32), pltpu.VMEM((rows, LANES), F32),
                        pltpu.VMEM((rows, HEAD_DIM), F32), pltpu.VMEM((tq, gw), F32)],
        compiler_params=_params(("arbitrary", "arbitrary", "arbitrary")),
        name="flash_sel_win")(q, kvb, kvb, kvb, kvb, sel, gates, oc)


def _shift_rows(x, d, fill):
    row = lax.broadcasted_iota(jnp.int32, x.shape, 0)
    return jnp.where(row < d, fill, pltpu.roll(x, d, axis=0))


def _lagged(u, tail, lag):
    us = pltpu.roll(u, lag, axis=0)
    row = lax.broadcasted_iota(jnp.int32, tail.shape, 0)
    top = jnp.where(row < lag, pltpu.roll(tail, lag, axis=0), us[:SUBLANES])
    return jnp.concatenate([top, us[SUBLANES:]], axis=0)


def _mix_tail(x, h, onsa, v_sc, v_lru, wgm_ref, wsco_ref, wlo_ref, wno_ref, wo_ref):
    d = x.shape[1]
    u_sc = _dot(v_sc.astype(BF16), wsco_ref[...])
    u_lru = _dot(v_lru.astype(BF16), wlo_ref[...])
    u_nsa = _dot(onsa, wno_ref[...])
    mix = jax.nn.sigmoid(_dot(h, wgm_ref[:, :d])) * u_nsa
    mix = mix + jax.nn.sigmoid(_dot(h, wgm_ref[:, d:2 * d])) * u_sc
    mix = mix + jax.nn.sigmoid(_dot(h, wgm_ref[:, 2 * d:])) * u_lru
    return x + _dot(mix.astype(BF16), wo_ref[...])


def _lru_coeffs(xc, wgate_ref, bgate_ref, lam_ref):
    w = xc.shape[1]
    gates = jax.nn.sigmoid(_dot(xc.astype(BF16), wgate_ref[...]) + bgate_ref[...])
    lam = -lam_ref[...]
    softplus = jnp.maximum(lam, 0.0) + jnp.log1p(jnp.exp(-jnp.abs(lam)))
    log_a = -LRU_C * gates[:, :w] * softplus
    a = jnp.exp(log_a)
    th = jnp.tanh(log_a)
    mult = jnp.sqrt(-2.0 * th / (1.0 - th))
    return a, mult, gates[:, w:]


def _mix_prompt_kernel(x_ref, onsa_ref, g_ref, wb_ref, wgm_ref, wscc_ref, wlc_ref, blc_ref,
                       wgate_ref, bgate_ref, lam_ref, wsco_ref, wlo_ref, wno_ref, wo_ref,
                       x1_ref, sct_ref, lct_ref, hl_ref, sc_tail, lc_tail, h_prev):
    t = pl.program_id(1)
    tm = x_ref.shape[0]
    w = sc_tail.shape[1]

    @pl.when(t == 0)
    def _():
        sc_tail[...] = jnp.zeros_like(sc_tail)
        lc_tail[...] = jnp.zeros_like(lc_tail)
        h_prev[...] = jnp.zeros_like(h_prev)

    x = x_ref[...]
    h = _rms(x, g_ref[...]).astype(BF16)
    proj = _dot(h, wb_ref[...])
    b_sc, c_sc, x_sc = proj[:, :w], proj[:, w:2 * w], proj[:, 2 * w:3 * w]
    x_lru, y_lru = proj[:, 3 * w:4 * w], proj[:, 4 * w:]

    cx = c_sc * x_sc
    wscc = wscc_ref[...]
    tail = sc_tail[...]
    conv = cx * wscc[SC_K - 1:SC_K]
    for lag in range(1, SC_K):
        conv = conv + _lagged(cx, tail, lag) * wscc[SC_K - 1 - lag:SC_K - lag]
    sc_tail[...] = cx[tm - SUBLANES:]
    sct_ref[...] = cx[tm - SUBLANES:]
    v_sc = b_sc * conv

    wlc = wlc_ref[...]
    tail = lc_tail[...]
    xc = x_lru * wlc[LRU_K - 1:LRU_K]
    for lag in range(1, LRU_K):
        xc = xc + _lagged(x_lru, tail, lag) * wlc[LRU_K - 1 - lag:LRU_K - lag]
    xc = xc + blc_ref[...]
    lc_tail[...] = x_lru[tm - SUBLANES:]
    lct_ref[...] = x_lru[tm - SUBLANES:]

    a, mult, gate_i = _lru_coeffs(xc, wgate_ref, bgate_ref, lam_ref)
    reset = (t * tm + lax.broadcasted_iota(jnp.int32, a.shape, 0)) == 0
    a = jnp.where(reset, 0.0, a)
    mult = jnp.where(reset, 1.0, mult)
    bx = mult * gate_i * xc
    d = 1
    while d < tm:
        bx = a * _shift_rows(bx, d, 0.0) + bx
        a = a * _shift_rows(a, d, 1.0)
        d *= 2
    hs = a * h_prev[...] + bx
    h_prev[...] = hs[tm - 1:]
    hl_ref[...] = hs[tm - SUBLANES:]
    v_lru = hs * jax.nn.gelu(y_lru)

    x1_ref[...] = _mix_tail(x, h, onsa_ref[...], v_sc, v_lru, wgm_ref, wsco_ref, wlo_ref, wno_ref, wo_ref)


def _mix_decode_kernel(x_ref, onsa_ref, scp_ref, lcp_ref, h0_ref, g_ref, wb_ref, wgm_ref, wscc_ref,
                       wlc_ref, blc_ref, wgate_ref, bgate_ref, lam_ref, wsco_ref, wlo_ref, wno_ref,
                       wo_ref, x1_ref, cx_ref, xl_ref, hn_ref):
    w = h0_ref.shape[1]
    x = x_ref[...]
    h = _rms(x, g_ref[...]).astype(BF16)
    proj = _dot(h, wb_ref[...])
    b_sc, c_sc, x_sc = proj[:, :w], proj[:, w:2 * w], proj[:, 2 * w:3 * w]
    x_lru, y_lru = proj[:, 3 * w:4 * w], proj[:, 4 * w:]

    cx = c_sc * x_sc
    wscc = wscc_ref[...]
    conv = cx * wscc[SC_K - 1:SC_K]
    for k in range(SC_K - 1):
        conv = conv + scp_ref[:, k * w:(k + 1) * w] * wscc[k:k + 1]
    cx_ref[...] = cx
    v_sc = b_sc * conv

    wlc = wlc_ref[...]
    xc = x_lru * wlc[LRU_K - 1:LRU_K]
    for k in range(LRU_K - 1):
        xc = xc + lcp_ref[:, k * w:(k + 1) * w] * wlc[k:k + 1]
    xc = xc + blc_ref[...]
    xl_ref[...] = x_lru

    a, mult, gate_i = _lru_coeffs(xc, wgate_ref, bgate_ref, lam_ref)
    hs = a * h0_ref[...] + mult * gate_i * xc
    hn_ref[...] = hs
    v_lru = hs * jax.nn.gelu(y_lru)

    x1_ref[...] = _mix_tail(x, h, onsa_ref[...], v_sc, v_lru, wgm_ref, wsco_ref, wlo_ref, wno_ref, wo_ref)


def _mix_weight_args(lw):
    return [lw["g1"], lw["wb"], lw["wgm"], lw["wscc"], lw["wlc"], lw["blc"], lw["wgate"], lw["bgate"],
            lw["lam"], lw["wsco"], lw["wlo"], lw["wno"], lw["wo"]]


def _mix_prompt(x, onsa, lw, batch, seq, tm):
    n, d = x.shape
    w = lw["wsco"].shape[0]
    nt = seq // tm
    row = lambda b, t: (b * nt + t, 0)
    per_b = lambda b, t: (b, 0, 0)
    wargs = _mix_weight_args(lw)
    tail_shape = jax.ShapeDtypeStruct((batch, SUBLANES, w), F32)
    tail_spec = pl.BlockSpec((None, SUBLANES, w), per_b)
    return pl.pallas_call(
        _mix_prompt_kernel,
        grid=(batch, nt),
        in_specs=[pl.BlockSpec((tm, d), row), pl.BlockSpec((tm, onsa.shape[1]), row)]
        + [_const_spec(a.shape) for a in wargs],
        out_specs=[pl.BlockSpec((tm, d), row), tail_spec, tail_spec, tail_spec],
        out_shape=[jax.ShapeDtypeStruct((n, d), F32), tail_shape, tail_shape, tail_shape],
        scratch_shapes=[pltpu.VMEM((SUBLANES, w), F32), pltpu.VMEM((SUBLANES, w), F32),
                        pltpu.VMEM((1, w), F32)],
        compiler_params=_params(("arbitrary", "arbitrary")),
        name="mix_prompt")(x, onsa, *wargs)


def _mix_decode(x, onsa, sc_prev, lc_prev, h0, lw):
    n, d = x.shape
    w = h0.shape[1]
    wargs = _mix_weight_args(lw)
    full = lambda a: pl.BlockSpec(a.shape, lambda i: (0,) * a.ndim)
    acts = [x, onsa, sc_prev, lc_prev, h0]
    state = jax.ShapeDtypeStruct((n, w), F32)
    return pl.pallas_call(
        _mix_decode_kernel,
        grid=(1,),
        in_specs=[full(a) for a in acts] + [_const_spec(a.shape) for a in wargs],
        out_specs=[pl.BlockSpec((n, d), lambda i: (0, 0))] + [pl.BlockSpec((n, w), lambda i: (0, 0))] * 3,
        out_shape=[jax.ShapeDtypeStruct((n, d), F32), state, state, state],
        compiler_params=_params(("arbitrary",)),
        name="mix_decode")(*acts, *wargs)


def _ffn_kernel(*refs, final, n_chunks):
    if final:
        x_ref, g_ref, wi_ref, wout_ref, gf_ref, y_ref = refs
    else:
        x_ref, g_ref, wi_ref, wout_ref, y_ref = refs
    x = x_ref[...]
    h = _rms(x, g_ref[...]).astype(BF16)
    d_ff = wout_ref.shape[0]
    ck = d_ff // n_chunks
    y = x
    for c in range(n_chunks):
        gate = _dot(h, wi_ref[:, c * ck:(c + 1) * ck])
        up = _dot(h, wi_ref[:, d_ff + c * ck:d_ff + (c + 1) * ck])
        y = y + _dot((jax.nn.silu(gate) * up).astype(BF16), wout_ref[c * ck:(c + 1) * ck, :])
    if final:
        y = _rms(y, gf_ref[...])
    y_ref[...] = y


def _ffn(x, g, wi, wout, gf, tm):
    n, d = x.shape
    final = gf is not None
    d_ff = wout.shape[0]
    n_chunks = 2 if d_ff % (2 * LANES) == 0 else 1
    row = lambda i: (i, 0)
    args = [x, g, wi, wout] + ([gf] if final else [])
    return pl.pallas_call(
        functools.partial(_ffn_kernel, final=final, n_chunks=n_chunks),
        grid=(n // tm,),
        in_specs=[pl.BlockSpec((tm, d), row)] + [_const_spec(a.shape) for a in args[1:]],
        out_specs=pl.BlockSpec((tm, d), row),
        out_shape=jax.ShapeDtypeStruct((n, d), F32),
        compiler_params=_params(("arbitrary",)),
        name="ffn")(*args)


def _group_onehot(rows, cols, row_div, col_div):
    r = lax.broadcasted_iota(jnp.int32, (rows, cols), 0) // row_div
    c = lax.broadcasted_iota(jnp.int32, (rows, cols), 1) // col_div
    return r == c


def _tile_lanes(q, reps):
    hd = q.shape[1]
    r = lax.broadcasted_iota(jnp.int32, (hd, reps * hd), 0)
    c = lax.broadcasted_iota(jnp.int32, (hd, reps * hd), 1)
    rep = jnp.where(c % hd == r, 1.0, 0.0).astype(BF16)
    return _dot(q, rep)


def _fold_heads(o):
    own = _group_onehot(o.shape[0], o.shape[1], Q_PER_KV, HEAD_DIM)
    o = jnp.where(own, o, 0.0)
    out = o[:, :HEAD_DIM]
    for g in range(1, o.shape[1] // HEAD_DIM):
        out = out + o[:, g * HEAD_DIM:(g + 1) * HEAD_DIM]
    return out


def _dec_cmp_kernel(pt_ref, q_ref, gate_ref, wct_ref, cache_ref, oc_ref, idx_ref, buf, sem,
                    *, row0, n_pages, page, past, n_sel):
    b = pl.program_id(0)
    nb = pl.num_programs(0)

    def copies(bb, slot):
        out = []
        for pg in range(n_pages):
            base = (row0 + pt_ref[bb * n_pages + pg]) * 4
            for kind in range(2):
                out.append(pltpu.make_async_copy(cache_ref.at[base + kind], buf.at[slot, kind, pg],
                                                 sem.at[slot]))
        return out

    slot = b % 2

    @pl.when(b == 0)
    def _():
        for c in copies(0, 0):
            c.start()

    @pl.when(b + 1 < nb)
    def _():
        for c in copies(b + 1, 1 - slot):
            c.start()

    for c in copies(b, slot):
        c.wait()

    nblk = past // CMP_BLOCK
    per_page = page // CMP_BLOCK
    npad = -(-(nblk + 1) // LANES) * LANES
    lane_blk = lax.broadcasted_iota(jnp.int32, (GROUP_W, page), 1) // CMP_BLOCK
    out_lane = lax.broadcasted_iota(jnp.int32, (GROUP_W, npad), 1)
    cmp_t = []
    for kind in range(2):
        w = wct_ref[kind]
        acc = jnp.zeros((GROUP_W, npad), F32)
        for pg in range(n_pages):
            y = buf[slot, kind, pg].reshape(GROUP_W, page) * w
            for j in range(per_page):
                part = jnp.sum(jnp.where(lane_blk == j, y, 0.0), axis=1, keepdims=True)
                acc = jnp.where(out_lane == pg * per_page + j, part, acc)
        cmp_t.append(acc.astype(BF16))
    kc_t, vc_t = cmp_t

    q = q_ref[...]
    own = _group_onehot(N_HEADS, GROUP_W, Q_PER_KV, HEAD_DIM)
    qbd = jnp.where(own, _tile_lanes(q, N_KV), 0.0).astype(BF16)
    s = _dot(qbd, kc_t)
    blk = lax.broadcasted_iota(jnp.int32, (N_HEADS, npad), 1)
    valid = (blk + 1) * CMP_BLOCK - 1 <= past
    s = jnp.where(valid, s, NEG)
    m = jnp.max(s, axis=-1, keepdims=True)
    p = jnp.where(valid, jnp.exp(s - m), 0.0)
    p = p / jnp.maximum(jnp.sum(p, axis=-1, keepdims=True), TINY)
    o = _fold_heads(_dot_nt(p.astype(BF16), vc_t))
    oc_ref[...] = gate_ref[:, 0:1] * o

    blk8 = lax.broadcasted_iota(jnp.int32, (SUBLANES, npad), 1)
    row8 = lax.broadcasted_iota(jnp.int32, (SUBLANES, npad), 0)
    imp = jnp.zeros((SUBLANES, npad), F32)
    for g in range(N_KV):
        psum = jnp.sum(p[g * Q_PER_KV:(g + 1) * Q_PER_KV], axis=0, keepdims=True)
        imp = jnp.where(row8 == g, psum, imp)
    cur = past // SEL_BLOCK
    imp = imp + jnp.where(blk8 == cur, FORCE, 0.0) + jnp.where(blk8 == 0, 0.5 * FORCE, 0.0)
    imp = jnp.where(blk8 <= cur, imp, -jnp.inf)
    blkf = blk8.astype(F32)
    idx = jnp.zeros((SUBLANES, LANES), F32)
    lane = lax.broadcasted_iota(jnp.int32, (SUBLANES, LANES), 1)
    for j in range(n_sel):
        mx = jnp.max(imp, axis=-1, keepdims=True)
        first = jnp.min(jnp.where(imp == mx, blkf, float(npad)), axis=-1, keepdims=True)
        idx = jnp.where(lane == j, first, idx)
        imp = jnp.where(blkf == first, -jnp.inf, imp)
    idx_ref[...] = idx.astype(jnp.int32)


def _dec_cmp(page_table, q3, gates3, wct, cache4, row0, past):
    db = q3.shape[0]
    n_pages = page_table.shape[1]
    page = cache4.shape[3]
    n_sel = min(N_SEL, past // SEL_BLOCK + 1)
    per_b = lambda b, pt: (b, 0, 0)
    grid_spec = pltpu.PrefetchScalarGridSpec(
        num_scalar_prefetch=1,
        grid=(db,),
        in_specs=[pl.BlockSpec((None, N_HEADS, HEAD_DIM), per_b),
                  pl.BlockSpec((None, N_HEADS, LANES), per_b),
                  pl.BlockSpec(wct.shape, lambda b, pt: (0, 0, 0)),
                  pl.BlockSpec(memory_space=pl.ANY)],
        out_specs=[pl.BlockSpec((None, N_HEADS, HEAD_DIM), per_b),
                   pl.BlockSpec((None, SUBLANES, LANES), per_b)],
        scratch_shapes=[pltpu.VMEM((2, 2, n_pages, N_KV, HEAD_DIM, page), F32),
                        pltpu.SemaphoreType.DMA((2,))])
    return pl.pallas_call(
        functools.partial(_dec_cmp_kernel, row0=row0, n_pages=n_pages, page=page, past=past, n_sel=n_sel),
        grid_spec=grid_spec,
        out_shape=[jax.ShapeDtypeStruct((db, N_HEADS, HEAD_DIM), F32),
                   jax.ShapeDtypeStruct((db, SUBLANES, LANES), jnp.int32)],
        compiler_params=_params(("arbitrary",)),
        name="dec_cmp_topk")(page_table.reshape(-1), q3, gates3, wct, cache4)


def _dec_attn_kernel(idx_ref, pt_ref, q_ref, gate_ref, oc_ref, kvn_ref, kwcol_ref, win_ref, cache_ref,
                     o_ref, wout_ref, kbuf, vbuf, sem, *, row0, n_sel, past, page):
    b = pl.program_id(0)
    nblk_past = past // SEL_BLOCK
    per_page = page // SEL_BLOCK
    n_pages = nblk_past // per_page
    win_len = win_ref.shape[1]

    def block_copies(g, j, n):
        base = (row0 + pt_ref[b * n_pages + n // per_page]) * 4
        dst = pl.ds(j * page, page)
        return (pltpu.make_async_copy(cache_ref.at[base + 2, g], kbuf.at[g, :, dst], sem.at[0]),
                pltpu.make_async_copy(cache_ref.at[base + 3, g], vbuf.at[g, :, dst], sem.at[1]))

    for g in range(N_KV):
        for j in range(n_sel):
            n = idx_ref[(b * N_KV + g) * n_sel + j]

            @pl.when(n < nblk_past)
            def _():
                for c in block_copies(g, j, n):
                    c.start()

            @pl.when(n >= nblk_past)
            def _():
                zeros = jnp.zeros((HEAD_DIM, page), F32)
                kbuf[g, :, j * page:(j + 1) * page] = zeros
                vbuf[g, :, j * page:(j + 1) * page] = zeros

    q = q_ref[...]
    own = _group_onehot(N_HEADS, GROUP_W, Q_PER_KV, HEAD_DIM)
    qbd_f = jnp.where(own, _tile_lanes(q, N_KV), 0.0)
    qbd = qbd_f.astype(BF16)
    gt = gate_ref[...]
    kvn = kvn_ref[...]

    win = win_ref[...]
    s = _dot(qbd, win[:GROUP_W].astype(BF16))
    colw = lax.broadcasted_iota(jnp.int32, s.shape, 1)
    s = jnp.where(win_len - colw < WINDOW, s, NEG)
    s_new = jnp.sum(qbd_f * kvn[:, 4 * GROUP_W:5 * GROUP_W], axis=-1, keepdims=True)
    m = jnp.maximum(jnp.max(s, axis=-1, keepdims=True), s_new)
    p = jnp.exp(s - m)
    p_new = jnp.exp(s_new - m)
    l = jnp.sum(p, axis=-1, keepdims=True) + p_new
    o_w = _fold_heads(_dot_nt(p.astype(BF16), win[GROUP_W:].astype(BF16))
                      + p_new * kvn[:, 5 * GROUP_W:]) / l

    lane_b = lax.broadcasted_iota(jnp.int32, kwcol_ref.shape, 1)
    new_col = jnp.sum(jnp.where(lane_b == b, kwcol_ref[...], 0.0), axis=1, keepdims=True)
    pos = lax.broadcasted_iota(jnp.int32, win.shape, 1)
    wout_ref[...] = jnp.where(pos == win_len - 1, new_col, pltpu.roll(win, win_len - 1, axis=1))

    for g in range(N_KV):
        for j in range(n_sel):
            n = idx_ref[(b * N_KV + g) * n_sel + j]

            @pl.when(n < nblk_past)
            def _():
                for c in block_copies(g, j, n):
                    c.wait()

    nkeys = n_sel * page
    rowg = lax.broadcasted_iota(jnp.int32, (N_HEADS, HEAD_DIM), 0) // Q_PER_KV
    rows = lax.broadcasted_iota(jnp.int32, (N_HEADS, nkeys), 0) // Q_PER_KV
    lanes = lax.broadcasted_iota(jnp.int32, (N_HEADS, nkeys), 1)
    colj = lanes // page
    in_page = (lanes % page) // SEL_BLOCK
    qf = q.astype(F32)
    s = jnp.zeros((N_HEADS, nkeys), F32)
    okay = jnp.zeros((N_HEADS, nkeys), F32)
    for g in range(N_KV):
        qg = jnp.where(rowg == g, qf, 0.0).astype(BF16)
        s = s + _dot(qg, kbuf[g].astype(BF16))
        for j in range(n_sel):
            n = idx_ref[(b * N_KV + g) * n_sel + j]
            flag = jnp.where(n < nblk_past, 1.0, 0.0)
            here = jnp.where(in_page == n % per_page, flag, 0.0)
            okay = jnp.where(rows == g, jnp.where(colj == j, here, okay), okay)
    s = jnp.where(okay > 0.5, s, NEG)
    s_new = jnp.sum(qbd_f * kvn[:, 2 * GROUP_W:3 * GROUP_W], axis=-1, keepdims=True)
    m = jnp.maximum(jnp.max(s, axis=-1, keepdims=True), s_new)
    p = jnp.where(okay > 0.5, jnp.exp(s - m), 0.0)
    p_new = jnp.exp(s_new - m)
    l = jnp.sum(p, axis=-1, keepdims=True) + p_new
    o_sel = jnp.zeros((N_HEADS, HEAD_DIM), F32)
    for g in range(N_KV):
        pg = jnp.where(rows == g, p, 0.0).astype(BF16)
        o_sel = o_sel + _dot_nt(pg, vbuf[g].astype(BF16))
    o_s = (o_sel + _fold_heads(p_new * jnp.where(own, kvn[:, 3 * GROUP_W:4 * GROUP_W], 0.0))) / l

    o_ref[...] = (oc_ref[...] + gt[:, 1:2] * o_s + gt[:, 2:3] * o_w).astype(BF16)


def _dec_attn(idx, page_table, q3, gates3, oc, kv_new, kw_cols, win, cache4, row0, win_row0, past):
    db = q3.shape[0]
    page = cache4.shape[3]
    n_sel = idx.shape[-1]
    per_b = lambda b, *_: (b, 0, 0)
    win_b = lambda b, *_: (win_row0 + b, 0, 0)
    win_block = (None,) + win.shape[1:]
    grid_spec = pltpu.PrefetchScalarGridSpec(
        num_scalar_prefetch=2,
        grid=(db,),
        in_specs=[pl.BlockSpec((None, N_HEADS, HEAD_DIM), per_b),
                  pl.BlockSpec((None, N_HEADS, LANES), per_b),
                  pl.BlockSpec((None, N_HEADS, HEAD_DIM), per_b),
                  pl.BlockSpec((None, 1, kv_new.shape[-1]), per_b),
                  pl.BlockSpec(kw_cols.shape, lambda b, *_: (0, 0)),
                  pl.BlockSpec(win_block, win_b),
                  pl.BlockSpec(memory_space=pl.ANY)],
        out_specs=[pl.BlockSpec((None, N_HEADS, HEAD_DIM), per_b),
                   pl.BlockSpec(win_block, per_b)],
        scratch_shapes=[pltpu.VMEM((N_KV, HEAD_DIM, n_sel * page), F32),
                        pltpu.VMEM((N_KV, HEAD_DIM, n_sel * page), F32),
                        pltpu.SemaphoreType.DMA((2,))])
    return pl.pallas_call(
        functools.partial(_dec_attn_kernel, row0=row0, n_sel=n_sel, past=past, page=page),
        grid_spec=grid_spec,
        out_shape=[jax.ShapeDtypeStruct((db, N_HEADS, HEAD_DIM), BF16),
                   jax.ShapeDtypeStruct((db,) + win.shape[1:], F32)],
        compiler_params=_params(("arbitrary",)),
        name="dec_attn")(idx.reshape(-1), page_table.reshape(-1), q3, gates3, oc,
                         kv_new[:, None, :], kw_cols, win, cache4)


def _layer_weights(l, norm1_g, w_in, w_cmp_k, w_cmp_v, w_nsa_out, w_sc_conv, w_sc_out, w_lru_conv,
                   b_lru_conv, w_lru_gate, b_lru_gate, lru_lambda, w_lru_out, w_o, norm2_g, w_ffn_in,
                   w_ffn_out):
    d = w_in.shape[1]
    nsa = N_HEADS * HEAD_DIM
    kvc = 2 * GROUP_W
    sc_w = w_sc_out.shape[1]
    lru_w = w_lru_out.shape[1]
    wi = w_in[l]
    o = 0
    wq = wi[:, o:o + nsa]; o += nsa
    wkv = wi[:, o:o + 3 * kvc]; o += 3 * kvc
    wg = wi[:, o:o + 3 * N_HEADS]; o += 3 * N_HEADS
    wb = wi[:, o:o + 3 * sc_w + 2 * lru_w]; o += 3 * sc_w + 2 * lru_w
    wgm = wi[:, o:]
    wkc = wkv[:, :kvc].reshape(d, 2, N_KV, HEAD_DIM).transpose(0, 2, 1, 3).reshape(d, kvc)
    pad_g = ((0, 0), (0, 0), (0, LANES - 3 * Q_PER_KV))
    wg_p = jnp.pad(wg.reshape(d, N_KV, 3 * Q_PER_KV), pad_g).reshape(d, N_KV * LANES)
    pad_h = ((0, 0), (0, 0), (0, LANES - 3))
    wg_s = jnp.pad(wg.reshape(d, N_HEADS, 3), pad_h).reshape(d, N_HEADS * LANES)
    ck, cv = w_cmp_k[l], w_cmp_v[l]
    wc_p = jnp.tile(jnp.concatenate([ck, cv], axis=1), (1, N_KV))
    wct = jnp.stack([jnp.tile(ck.T, (N_KV, 2)), jnp.tile(cv.T, (N_KV, 2))])
    eye = jnp.eye(LRU_HEADS, dtype=F32)
    wgate = jnp.concatenate(
        [(eye[:, None, :, None] * w_lru_gate[l, k][:, :, None, :]).reshape(lru_w, lru_w) for k in range(2)],
        axis=1)
    bf = lambda a: a.astype(BF16)
    return {
        "g1": norm1_g[l][None], "wq": bf(wq), "wkvt": bf(wkv.T), "wkv": bf(wkv), "wkc": bf(wkc),
        "wg_p": bf(wg_p), "wg_s": bf(wg_s), "wc_p": wc_p, "wct": wct,
        "wb": bf(wb), "wgm": bf(wgm), "wscc": w_sc_conv[l], "wlc": w_lru_conv[l],
        "blc": b_lru_conv[l][None], "wgate": bf(wgate), "bgate": b_lru_gate[l].reshape(1, 2 * lru_w),
        "lam": lru_lambda[l][None], "wsco": bf(w_sc_out[l]), "wlo": bf(w_lru_out[l]),
        "wno": bf(w_nsa_out[l]), "wo": bf(w_o[l]), "g2": norm2_g[l][None],
        "wfi": bf(w_ffn_in[l]), "wfo": bf(w_ffn_out[l]),
    }


def _pick_tile(n, pref):
    t = min(n, pref)
    while n % t:
        t //= 2
    return t


def kernel(x_prompt, x_sample, cache_kv, cache_win, state_sconv, state_lru_conv, state_lru_h, page_table, norm1_g, w_in, w_cmp_k, w_cmp_v, w_nsa_out, w_sc_conv, w_sc_out, w_lru_conv, b_lru_conv, w_lru_gate, b_lru_gate, lru_lambda, w_lru_out, w_o, norm2_g, w_ffn_in, w_ffn_out, final_g):
    batch, seq, d = x_prompt.shape
    db, dec_seq, _ = x_sample.shape
    depth = w_in.shape[0]
    n_pool, page = cache_kv.shape[1], cache_kv.shape[2]
    past = page_table.shape[1] * page
    win_len = cache_win.shape[2]
    assert dec_seq == 1 and past % SEL_BLOCK == 0 and page % SEL_BLOCK == 0
    assert seq % 256 == 0 and page == LANES

    tq = 256
    tm_qkv = _pick_tile(seq, 512)
    tm_mix = _pick_tile(seq, 256)
    tm_ffn = _pick_tile(batch * seq, 512)
    gf = final_g[None]

    cache4 = jnp.transpose(cache_kv, (0, 1, 3, 4, 5, 2)).reshape(depth * n_pool * 4, N_KV, HEAD_DIM, page)
    win_t = jnp.transpose(cache_win, (0, 1, 3, 4, 5, 2)).reshape(depth * db, 2 * GROUP_W, win_len)

    xp = x_prompt.reshape(batch * seq, d)
    xs = x_sample.reshape(db, d)
    outs_p, outs_s = [], []
    for l in range(depth):
        lw = _layer_weights(l, norm1_g, w_in, w_cmp_k, w_cmp_v, w_nsa_out, w_sc_conv, w_sc_out,
                            w_lru_conv, b_lru_conv, w_lru_gate, b_lru_gate, lru_lambda, w_lru_out,
                            w_o, norm2_g, w_ffn_in, w_ffn_out)
        last = l == depth - 1
        q, kvt, kwt, gates, kvb, cmpk = _qkv_proj(xp, lw["g1"], lw["wq"], lw["wkvt"], lw["wg_p"],
                                                  (lw["wkc"], lw["wc_p"]), batch, seq, tm_qkv, tk=tq)
        oc, sel = _cmp_topk(q, cmpk, gates, batch, seq, tq)
        onsa = _flash(q, kvb, sel, gates, oc, batch, seq, tq, rc=128)
        x1, sct, lct, hl = _mix_prompt(xp, onsa, lw, batch, seq, tm_mix)
        xp = _ffn(x1, lw["g2"], lw["wfi"], lw["wfo"], gf if last else None, tm_ffn)
        w_keep = min(WINDOW, seq)
        outs_p.append((kvt, kwt[:, :, seq - w_keep:], sct[:, SUBLANES - (SC_K - 1):],
                       lct[:, SUBLANES - (LRU_K - 1):], hl[:, SUBLANES - 1]))
        q, kvt, kwt, gates, kvr = _qkv_proj(xs, lw["g1"], lw["wq"], lw["wkvt"], lw["wg_s"],
                                            (lw["wkv"],), 1, db, db)
        q3 = q.reshape(db, N_HEADS, HEAD_DIM)
        gates3 = gates.reshape(db, N_HEADS, LANES)
        oc, idx = _dec_cmp(page_table, q3, gates3, lw["wct"], cache4, l * n_pool, past)
        n_sel = min(N_SEL, past // SEL_BLOCK + 1)
        idx = idx[:, :N_KV, :n_sel]
        onsa, win_new = _dec_attn(idx, page_table, q3, gates3, oc, kvr, kwt[0], win_t, cache4,
                                  l * n_pool, l * db, past)
        sc_prev = state_sconv[l].reshape(db, -1)
        lc_prev = state_lru_conv[l].reshape(db, -1)
        x1, cx, xl, hn = _mix_decode(xs, onsa.reshape(db, N_HEADS * HEAD_DIM), sc_prev, lc_prev,
                                     state_lru_h[l], lw)
        xs = _ffn(x1, lw["g2"], lw["wfi"], lw["wfo"], gf if last else None, db)
        outs_s.append((kvt, win_new,
                       jnp.concatenate([state_sconv[l][:, 1:], cx[:, None]], axis=1),
                       jnp.concatenate([state_lru_conv[l][:, 1:], xl[:, None]], axis=1),
                       hn))
    stack = lambda outs, i: jnp.stack([o[i] for o in outs])
    rows_out = lambda a, kinds: jnp.transpose(
        a.reshape(a.shape[:2] + (kinds, N_KV, HEAD_DIM, a.shape[-1])), (0, 1, 5, 2, 3, 4))
    y_prompt = xp.reshape(batch, seq, d)
    y_sample = xs.reshape(db, 1, d)
    kv_s = jnp.transpose(stack(outs_s, 0).reshape(depth, 1, 4, N_KV, HEAD_DIM, db), (0, 5, 1, 2, 3, 4))
    return (y_prompt, y_sample,
            rows_out(stack(outs_p, 0), 4), rows_out(stack(outs_p, 1), 2),
            stack(outs_p, 2), stack(outs_p, 3), stack(outs_p, 4),
            kv_s, rows_out(stack(outs_s, 1), 2),
            stack(outs_s, 2), stack(outs_s, 3), stack(outs_s, 4))
```

```python
import functools

import jax
import jax.numpy as jnp
from jax import lax
from jax.experimental import pallas as pl
from jax.experimental.pallas import tpu as pltpu

N_HEADS = 16
N_KV = 4
HEAD_DIM = 64
Q_PER_KV = N_HEADS // N_KV
CMP_BLOCK = 64
SEL_BLOCK = CMP_BLOCK
N_SEL = 8
WINDOW = 512
SC_K = 3
LRU_K = 4
LRU_HEADS = 8
LRU_C = 8.0
SCALE = HEAD_DIM ** -0.5
LOG2E = 1.4426950408889634
EPS = 1e-6
NEG = -1e30
TINY = 1e-20
FORCE = 1e6

LANES = 128
SUBLANES = 8
VMEM_LIMIT = 56 * 1024 * 1024
GROUP_W = N_KV * HEAD_DIM

F32 = jnp.float32
BF16 = jnp.bfloat16
_NT = (((1,), (1,)), ((), ()))


def _params(sem):
    return pltpu.CompilerParams(dimension_semantics=sem, vmem_limit_bytes=VMEM_LIMIT)


def _const_spec(shape):
    nd = len(shape)
    return pl.BlockSpec(shape, lambda *_: (0,) * nd, pipeline_mode=pl.Buffered(1))


def _rms(x, g):
    y = x * lax.rsqrt(jnp.mean(x * x, axis=-1, keepdims=True) + EPS)
    return y * g


def _dot(a, b):
    return jnp.dot(a, b, preferred_element_type=F32)


def _dot_nt(a, b):
    return lax.dot_general(a, b, _NT, preferred_element_type=F32)


def _qkv_prompt_kernel(x_ref, g_ref, wqt_ref, wkvt_ref, wgt_ref, wkp_ref, wkc_ref, wc_ref,
                       qt_ref, kvt_ref, kwt_ref, gt_ref, kp_ref, vt_ref, cmp_ref, *, tk):
    tm = x_ref.shape[0]
    h = _rms(x_ref[...], g_ref[...]).astype(BF16)
    qt_ref[...] = (_dot_nt(wqt_ref[...], h) * (SCALE * LOG2E)).astype(BF16)
    kvt = _dot_nt(wkvt_ref[...], h)
    n_rows = kvt_ref.shape[0]
    kvt_ref[...] = kvt[:n_rows]
    kwt_ref[...] = kvt[n_rows:]
    gt_ref[...] = jax.nn.sigmoid(_dot_nt(wgt_ref[...], h))
    kp_ref[...] = _dot(h, wkp_ref[...]).astype(BF16)
    v_sel = kvt[3 * GROUP_W:4 * GROUP_W]
    v_win = kvt[5 * GROUP_W:]
    for c in range(tm // tk):
        cs = slice(c * tk, (c + 1) * tk)
        vt_ref[c, :GROUP_W] = v_sel[:, cs].astype(BF16)
        vt_ref[c, GROUP_W:] = v_win[:, cs].astype(BF16)
    width = wc_ref.shape[1]
    blocks = _dot(h, wkc_ref[...]).reshape(tm // CMP_BLOCK, CMP_BLOCK, width)
    cmp_ref[...] = jnp.sum(blocks * wc_ref[...][None], axis=1)


def _qkv_prompt(x, lw, batch, seq, tm, tk):
    n, d = x.shape
    nt = seq // tm
    n_rows = 4 * GROUP_W
    n_win = 2 * GROUP_W
    nq = N_HEADS * HEAD_DIM
    row = lambda b, t: (b * nt + t, 0)
    colblk = lambda b, t: (b, 0, t)
    args = [x, lw["g1"], lw["wqt"], lw["wkvt"], lw["wgt"], lw["wkp"], lw["wkc"], lw["wc_p"]]
    n_gate = lw["wgt"].shape[0]
    n_kp = lw["wkp"].shape[1]
    n_cmp = lw["wc_p"].shape[1]
    return pl.pallas_call(
        functools.partial(_qkv_prompt_kernel, tk=tk),
        grid=(batch, nt),
        in_specs=[pl.BlockSpec((tm, d), row)] + [_const_spec(a.shape) for a in args[1:]],
        out_specs=[pl.BlockSpec((None, nq, tm), colblk),
                   pl.BlockSpec((None, n_rows, tm), colblk),
                   pl.BlockSpec((None, n_win, tm), colblk),
                   pl.BlockSpec((None, n_gate, tm), colblk),
                   pl.BlockSpec((tm, n_kp), row),
                   pl.BlockSpec((None, tm // tk, n_win, tk), lambda b, t: (b, t, 0, 0)),
                   pl.BlockSpec((tm // CMP_BLOCK, n_cmp), row)],
        out_shape=[jax.ShapeDtypeStruct((batch, nq, seq), BF16),
                   jax.ShapeDtypeStruct((batch, n_rows, seq), F32),
                   jax.ShapeDtypeStruct((batch, n_win, seq), F32),
                   jax.ShapeDtypeStruct((batch, n_gate, seq), F32),
                   jax.ShapeDtypeStruct((n, n_kp), BF16),
                   jax.ShapeDtypeStruct((batch, seq // tk, n_win, tk), BF16),
                   jax.ShapeDtypeStruct((n // CMP_BLOCK, n_cmp), F32)],
        compiler_params=_params(("arbitrary", "arbitrary")), name="qkv_prompt")(*args)


def _qkv_decode_kernel(x_ref, g_ref, wq_ref, wkvt_ref, wg_ref, wkv_ref,
                       q_ref, kvt_ref, kwt_ref, gate_ref, kvr_ref):
    h = _rms(x_ref[...], g_ref[...]).astype(BF16)
    q_ref[...] = (_dot(h, wq_ref[...]) * SCALE).astype(BF16)
    kvt = _dot_nt(wkvt_ref[...], h)
    n_rows = kvt_ref.shape[0]
    kvt_ref[...] = kvt[:n_rows]
    kwt_ref[...] = kvt[n_rows:]
    gate_ref[...] = jax.nn.sigmoid(_dot(h, wg_ref[...]))
    kvr_ref[...] = _dot(h, wkv_ref[...])


def _qkv_decode(x, lw):
    n, d = x.shape
    args = [x, lw["g1"], lw["wq"], lw["wkvt"], lw["wg_s"], lw["wkv"]]
    full = lambda shape: pl.BlockSpec(shape, lambda i: (0,) * len(shape))
    shapes = [((n, lw["wq"].shape[1]), BF16), ((4 * GROUP_W, n), F32), ((2 * GROUP_W, n), F32),
              ((n, lw["wg_s"].shape[1]), F32), ((n, lw["wkv"].shape[1]), F32)]
    return pl.pallas_call(
        _qkv_decode_kernel,
        grid=(1,),
        in_specs=[full(x.shape)] + [_const_spec(a.shape) for a in args[1:]],
        out_specs=[full(s) for s, _ in shapes],
        out_shape=[jax.ShapeDtypeStruct(s, t) for s, t in shapes],
        compiler_params=_params(("arbitrary",)), name="qkv_decode")(*args)


def _cmp_topk_kernel(qt_ref, cmp_ref, cmpt_ref, gt_ref, oct_ref, selt_ref, qpad, *, n_sel):
    tq = qt_ref.shape[1]
    nblk = cmp_ref.shape[0]
    t0 = pl.program_id(2) * tq
    kcb = cmp_ref[...].astype(BF16)
    kct = cmpt_ref[...].astype(BF16)
    blk = lax.broadcasted_iota(jnp.int32, (nblk, tq), 0)
    tpos = t0 + lax.broadcasted_iota(jnp.int32, (nblk, tq), 1)
    valid = (blk + 1) * CMP_BLOCK - 1 <= tpos
    gt = gt_ref[...]
    qpad[HEAD_DIM:, :] = jnp.zeros((HEAD_DIM, tq), BF16)
    imp = jnp.zeros((nblk, tq), F32)
    for r in range(Q_PER_KV):
        hs = slice(r * HEAD_DIM, (r + 1) * HEAD_DIM)
        qpad[:HEAD_DIM, :] = qt_ref[hs, :]
        s = _dot(kcb, qpad[...])
        s = jnp.where(valid, s, NEG)
        m = jnp.max(s, axis=0, keepdims=True)
        p = jnp.where(valid, jnp.exp2(s - m), 0.0)
        p = p / jnp.maximum(jnp.sum(p, axis=0, keepdims=True), TINY)
        imp = imp + p
        o = _dot(kct, p.astype(BF16))
        oct_ref[hs, :] = gt[3 * r:3 * r + 1, :] * o[HEAD_DIM:, :]
    cur = lax.shift_right_logical(tpos, 6)
    imp = imp + jnp.where(blk == cur, FORCE, 0.0) + jnp.where(blk == 0, 0.5 * FORCE, 0.0)
    imp = jnp.where(blk <= cur, imp, -FORCE)
    blkf = blk.astype(F32)
    sel = jnp.zeros((nblk, tq), F32)
    for _ in range(n_sel):
        mx = jnp.max(imp, axis=0, keepdims=True)
        first = jnp.min(jnp.where(imp == mx, blkf, float(nblk)), axis=0, keepdims=True)
        hit = blkf == first
        sel = jnp.where(hit, 1.0, sel)
        imp = jnp.where(hit, -jnp.inf, imp)
    selt_ref[...] = jnp.where(blk <= cur, sel, 0.0).astype(BF16)


def _cmp_topk(qt, cmpk, cmpt, gates_t, batch, seq, tq):
    nblk = seq // CMP_BLOCK
    nq = seq // tq
    gw = Q_PER_KV * HEAD_DIM
    n_gate = gates_t.shape[1] // N_KV
    tile = lambda b, g, i: (b, g, i)
    return pl.pallas_call(
        functools.partial(_cmp_topk_kernel, n_sel=min(N_SEL, nblk)),
        grid=(batch, N_KV, nq),
        in_specs=[pl.BlockSpec((None, gw, tq), tile),
                  pl.BlockSpec((nblk, LANES), lambda b, g, i: (b, g)),
                  pl.BlockSpec((None, None, LANES, nblk), lambda b, g, i: (b, g, 0, 0)),
                  pl.BlockSpec((None, n_gate, tq), tile)],
        out_specs=[pl.BlockSpec((None, gw, tq), tile),
                   pl.BlockSpec((None, None, nblk, tq), lambda b, g, i: (g, b, 0, i))],
        out_shape=[jax.ShapeDtypeStruct((batch, N_HEADS * HEAD_DIM, seq), F32),
                   jax.ShapeDtypeStruct((N_KV, batch, nblk, seq), BF16)],
        scratch_shapes=[pltpu.VMEM((LANES, tq), BF16)],
        compiler_params=_params(("arbitrary", "arbitrary", "arbitrary")),
        name="cmp_topk")(qt, cmpk, cmpt, gates_t)


def _flash_kernel(qt_ref, kp_ref, vs_ref, vw_ref, selt_ref, gt_ref, oct_ref, o_ref,
                  qs_ref, qw_ref, bias_ref, s_ref, m_ref, l_ref, acc_ref, out_ref, *, n_back, qc):
    tq = qt_ref.shape[1]
    nblk = selt_ref.shape[0]
    tk = vs_ref.shape[2]
    i = pl.program_id(2)
    t0 = i * tq
    cols = Q_PER_KV * tq
    zeros = jnp.zeros((HEAD_DIM, cols), BF16)
    qs_ref[HEAD_DIM:, :] = zeros
    qw_ref[:HEAD_DIM, :] = zeros
    for r in range(Q_PER_KV):
        qh = qt_ref[r * HEAD_DIM:(r + 1) * HEAD_DIM, :]
        qs_ref[:HEAD_DIM, r * tq:(r + 1) * tq] = qh
        qw_ref[HEAD_DIM:, r * tq:(r + 1) * tq] = qh
    key0 = lax.broadcasted_iota(jnp.int32, (tk, tq), 0)
    tpos = t0 + lax.broadcasted_iota(jnp.int32, (tk, tq), 1)
    e_blk = lax.shift_right_logical(lax.broadcasted_iota(jnp.int32, (tk, nblk), 0), 6)
    e_col = lax.broadcasted_iota(jnp.int32, (tk, nblk), 1)

    def reset():
        m_ref[...] = jnp.full(m_ref.shape, NEG, F32)
        l_ref[...] = jnp.zeros(l_ref.shape, F32)
        acc_ref[...] = jnp.zeros(acc_ref.shape, F32)

    def key_tile(kt):
        return kp_ref[pl.ds(pl.multiple_of(kt * tk, tk), tk), :]

    def fill_scores(q_ref, kt):
        kp = key_tile(kt)
        for c in range(cols // qc):
            b0 = (c * qc) % tq
            s_ref[:, c * qc:(c + 1) * qc] = _dot(kp, q_ref[:, c * qc:(c + 1) * qc]) + bias_ref[:, b0:b0 + qc]

    def attend(q_ref, v_ref, kt, kt_next):
        kp = key_tile(kt_next)
        vt = v_ref[kt]
        pending = None
        for c in range(cols // qc):
            ls = slice(c * qc, (c + 1) * qc)
            b0 = (c * qc) % tq
            s = s_ref[:, ls]
            s_ref[:, ls] = _dot(kp, q_ref[:, ls]) + bias_ref[:, b0:b0 + qc]
            m_old = m_ref[:, ls]
            m_new = jnp.maximum(m_old, jnp.max(s, axis=0, keepdims=True))
            alpha = jnp.exp2(m_old - m_new)
            p = jnp.exp2(s - m_new)
            l_ref[:, ls] = alpha * l_ref[:, ls] + jnp.sum(p, axis=0, keepdims=True)
            m_ref[:, ls] = m_new
            pv = _dot(vt, p.astype(BF16))
            if pending is not None:
                pls, palpha, ppv = pending
                acc_ref[:, pls] = palpha * acc_ref[:, pls] + ppv
            pending = (ls, alpha, pv)
        pls, palpha, ppv = pending
        acc_ref[:, pls] = palpha * acc_ref[:, pls] + ppv

    def emit(k):
        gt = gt_ref[...]
        for r in range(Q_PER_KV):
            ls = slice(r * tq, (r + 1) * tq)
            hs = slice(r * HEAD_DIM, (r + 1) * HEAD_DIM)
            scale = gt[3 * r + k:3 * r + k + 1, :] / l_ref[:, ls]
            out_ref[hs, :] = out_ref[hs, :] + scale * acc_ref[:, ls]

    out_ref[...] = oct_ref[...]

    def sel_bias(kt):
        expand = jnp.where(e_col == kt * (tk // SEL_BLOCK) + e_blk, 1.0, 0.0).astype(BF16)
        chosen = _dot(expand, selt_ref[...])
        key = kt * tk + key0
        bias_ref[...] = jnp.where(chosen > 0.5, jnp.where(key <= tpos, 0.0, NEG), NEG)

    def sel_body(kt, carry):
        kt_next = jnp.minimum(kt + 1, i)
        sel_bias(kt_next)
        attend(qs_ref, vs_ref, kt, kt_next)
        return carry

    reset()
    sel_bias(0)
    fill_scores(qs_ref, 0)
    lax.fori_loop(0, i + 1, sel_body, 0)
    emit(1)

    def win_bias(kt):
        dlt = tpos - (kt * tk + key0)
        bias_ref[...] = jnp.where(dlt >= 0, jnp.where(dlt < WINDOW, 0.0, NEG), NEG)

    n_win = jnp.minimum(i, n_back) + 1

    def win_body(j, carry):
        kt = i - j
        kt_next = i - jnp.minimum(j + 1, n_win - 1)
        win_bias(kt_next)
        attend(qw_ref, vw_ref, kt, kt_next)
        return carry

    reset()
    win_bias(i)
    fill_scores(qw_ref, i)
    lax.fori_loop(0, n_win, win_body, 0)
    emit(2)
    o_ref[...] = out_ref[...].T.astype(BF16)


def _flash(qt, kp, vt, selt, gates_t, oct, batch, seq, tq, qc):
    nblk = seq // SEL_BLOCK
    nq = seq // tq
    nkt, tk = vt.shape[1], vt.shape[3]
    assert tk == tq
    gw = Q_PER_KV * HEAD_DIM
    n_gate = gates_t.shape[1] // N_KV
    n_back = -(-(WINDOW - 1) // tk)
    tile = lambda b, g, i: (b, g, i)
    v_spec = lambda kind: pl.BlockSpec((None, nkt, HEAD_DIM, tk),
                                       lambda b, g, i: (b, 0, kind * N_KV + g, 0))
    cols = Q_PER_KV * tq
    return pl.pallas_call(
        functools.partial(_flash_kernel, n_back=n_back, qc=qc),
        grid=(batch, N_KV, nq),
        in_specs=[pl.BlockSpec((None, gw, tq), tile),
                  pl.BlockSpec((seq, LANES), lambda b, g, i: (b, g)),
                  v_spec(0), v_spec(1),
                  pl.BlockSpec((None, None, nblk, tq), lambda b, g, i: (g, b, 0, i)),
                  pl.BlockSpec((None, n_gate, tq), tile),
                  pl.BlockSpec((None, gw, tq), tile)],
        out_specs=pl.BlockSpec((tq, gw), lambda b, g, i: (b * nq + i, g)),
        out_shape=jax.ShapeDtypeStruct((batch * seq, N_HEADS * HEAD_DIM), BF16),
        scratch_shapes=[pltpu.VMEM((LANES, cols), BF16), pltpu.VMEM((LANES, cols), BF16),
                        pltpu.VMEM((tk, tq), F32), pltpu.VMEM((tk, cols), F32),
                        pltpu.VMEM((1, cols), F32), pltpu.VMEM((1, cols), F32),
                        pltpu.VMEM((HEAD_DIM, cols), F32), pltpu.VMEM((gw, tq), F32)],
        compiler_params=_params(("arbitrary", "arbitrary", "arbitrary")),
        name="flash_sel_win")(qt, kp, vt, vt, selt, gates_t, oct)


def _shift_rows(x, d, fill):
    row = lax.broadcasted_iota(jnp.int32, x.shape, 0)
    return jnp.where(row < d, fill, pltpu.roll(x, d, axis=0))


def _lagged(u, tail, lag):
    us = pltpu.roll(u, lag, axis=0)
    row = lax.broadcasted_iota(jnp.int32, tail.shape, 0)
    top = jnp.where(row < lag, pltpu.roll(tail, lag, axis=0), us[:SUBLANES])
    return jnp.concatenate([top, us[SUBLANES:]], axis=0)


def _mix_tail(x, h, onsa, v_sc, v_lru, wgm_ref, wsco_ref, wlo_ref, wno_ref, wo_ref):
    d = x.shape[1]
    u_sc = _dot(v_sc.astype(BF16), wsco_ref[...])
    u_lru = _dot(v_lru.astype(BF16), wlo_ref[...])
    u_nsa = _dot(onsa, wno_ref[...])
    mix = jax.nn.sigmoid(_dot(h, wgm_ref[:, :d])) * u_nsa
    mix = mix + jax.nn.sigmoid(_dot(h, wgm_ref[:, d:2 * d])) * u_sc
    mix = mix + jax.nn.sigmoid(_dot(h, wgm_ref[:, 2 * d:])) * u_lru
    return x + _dot(mix.astype(BF16), wo_ref[...])


def _lru_coeffs(xc, wgate_ref, bgate_ref, lam_ref):
    w = xc.shape[1]
    gates = jax.nn.sigmoid(_dot(xc.astype(BF16), wgate_ref[...]) + bgate_ref[...])
    lam = -lam_ref[...]
    softplus = jnp.maximum(lam, 0.0) + jnp.log1p(jnp.exp(-jnp.abs(lam)))
    log_a = -LRU_C * gates[:, :w] * softplus
    a = jnp.exp(log_a)
    th = jnp.tanh(log_a)
    mult = jnp.sqrt(-2.0 * th / (1.0 - th))
    return a, mult, gates[:, w:]


def _mix_prompt_kernel(x_ref, onsa_ref, g_ref, wb_ref, wgm_ref, wscc_ref, wlc_ref, blc_ref,
                       wgate_ref, bgate_ref, lam_ref, wsco_ref, wlo_ref, wno_ref, wo_ref,
                       x1_ref, sct_ref, lct_ref, hl_ref, sc_tail, lc_tail, h_prev):
    t = pl.program_id(1)
    tm = x_ref.shape[0]
    w = sc_tail.shape[1]

    @pl.when(t == 0)
    def _():
        sc_tail[...] = jnp.zeros_like(sc_tail)
        lc_tail[...] = jnp.zeros_like(lc_tail)
        h_prev[...] = jnp.zeros_like(h_prev)

    x = x_ref[...]
    h = _rms(x, g_ref[...]).astype(BF16)
    proj = _dot(h, wb_ref[...])
    b_sc, c_sc, x_sc = proj[:, :w], proj[:, w:2 * w], proj[:, 2 * w:3 * w]
    x_lru, y_lru = proj[:, 3 * w:4 * w], proj[:, 4 * w:]

    cx = c_sc * x_sc
    wscc = wscc_ref[...]
    tail = sc_tail[...]
    conv = cx * wscc[SC_K - 1:SC_K]
    for lag in range(1, SC_K):
        conv = conv + _lagged(cx, tail, lag) * wscc[SC_K - 1 - lag:SC_K - lag]
    sc_tail[...] = cx[tm - SUBLANES:]
    sct_ref[...] = cx[tm - SUBLANES:]
    v_sc = b_sc * conv

    wlc = wlc_ref[...]
    tail = lc_tail[...]
    xc = x_lru * wlc[LRU_K - 1:LRU_K]
    for lag in range(1, LRU_K):
        xc = xc + _lagged(x_lru, tail, lag) * wlc[LRU_K - 1 - lag:LRU_K - lag]
    xc = xc + blc_ref[...]
    lc_tail[...] = x_lru[tm - SUBLANES:]
    lct_ref[...] = x_lru[tm - SUBLANES:]

    a, mult, gate_i = _lru_coeffs(xc, wgate_ref, bgate_ref, lam_ref)
    reset = (t * tm + lax.broadcasted_iota(jnp.int32, a.shape, 0)) == 0
    a = jnp.where(reset, 0.0, a)
    mult = jnp.where(reset, 1.0, mult)
    bx = mult * gate_i * xc
    d = 1
    while d < tm:
        bx = a * _shift_rows(bx, d, 0.0) + bx
        a = a * _shift_rows(a, d, 1.0)
        d *= 2
    hs = a * h_prev[...] + bx
    h_prev[...] = hs[tm - 1:]
    hl_ref[...] = hs[tm - SUBLANES:]
    v_lru = hs * jax.nn.gelu(y_lru)

    x1_ref[...] = _mix_tail(x, h, onsa_ref[...], v_sc, v_lru, wgm_ref, wsco_ref, wlo_ref, wno_ref, wo_ref)


def _mix_decode_kernel(x_ref, onsa_ref, scp_ref, lcp_ref, h0_ref, g_ref, wb_ref, wgm_ref, wscc_ref,
                       wlc_ref, blc_ref, wgate_ref, bgate_ref, lam_ref, wsco_ref, wlo_ref, wno_ref,
                       wo_ref, x1_ref, cx_ref, xl_ref, hn_ref):
    w = h0_ref.shape[1]
    x = x_ref[...]
    h = _rms(x, g_ref[...]).astype(BF16)
    proj = _dot(h, wb_ref[...])
    b_sc, c_sc, x_sc = proj[:, :w], proj[:, w:2 * w], proj[:, 2 * w:3 * w]
    x_lru, y_lru = proj[:, 3 * w:4 * w], proj[:, 4 * w:]

    cx = c_sc * x_sc
    wscc = wscc_ref[...]
    conv = cx * wscc[SC_K - 1:SC_K]
    for k in range(SC_K - 1):
        conv = conv + scp_ref[:, k * w:(k + 1) * w] * wscc[k:k + 1]
    cx_ref[...] = cx
    v_sc = b_sc * conv

    wlc = wlc_ref[...]
    xc = x_lru * wlc[LRU_K - 1:LRU_K]
    for k in range(LRU_K - 1):
        xc = xc + lcp_ref[:, k * w:(k + 1) * w] * wlc[k:k + 1]
    xc = xc + blc_ref[...]
    xl_ref[...] = x_lru

    a, mult, gate_i = _lru_coeffs(xc, wgate_ref, bgate_ref, lam_ref)
    hs = a * h0_ref[...] + mult * gate_i * xc
    hn_ref[...] = hs
    v_lru = hs * jax.nn.gelu(y_lru)

    x1_ref[...] = _mix_tail(x, h, onsa_ref[...], v_sc, v_lru, wgm_ref, wsco_ref, wlo_ref, wno_ref, wo_ref)


def _mix_weight_args(lw):
    return [lw["g1"], lw["wb"], lw["wgm"], lw["wscc"], lw["wlc"], lw["blc"], lw["wgate"], lw["bgate"],
            lw["lam"], lw["wsco"], lw["wlo"], lw["wno"], lw["wo"]]


def _mix_prompt(x, onsa, lw, batch, seq, tm):
    n, d = x.shape
    w = lw["wsco"].shape[0]
    nt = seq // tm
    row = lambda b, t: (b * nt + t, 0)
    per_b = lambda b, t: (b, 0, 0)
    wargs = _mix_weight_args(lw)
    tail_shape = jax.ShapeDtypeStruct((batch, SUBLANES, w), F32)
    tail_spec = pl.BlockSpec((None, SUBLANES, w), per_b)
    return pl.pallas_call(
        _mix_prompt_kernel,
        grid=(batch, nt),
        in_specs=[pl.BlockSpec((tm, d), row), pl.BlockSpec((tm, onsa.shape[1]), row)]
        + [_const_spec(a.shape) for a in wargs],
        out_specs=[pl.BlockSpec((tm, d), row), tail_spec, tail_spec, tail_spec],
        out_shape=[jax.ShapeDtypeStruct((n, d), F32), tail_shape, tail_shape, tail_shape],
        scratch_shapes=[pltpu.VMEM((SUBLANES, w), F32), pltpu.VMEM((SUBLANES, w), F32),
                        pltpu.VMEM((1, w), F32)],
        compiler_params=_params(("arbitrary", "arbitrary")),
        name="mix_prompt")(x, onsa, *wargs)


def _mix_decode(x, onsa, sc_prev, lc_prev, h0, lw):
    n, d = x.shape
    w = h0.shape[1]
    wargs = _mix_weight_args(lw)
    full = lambda a: pl.BlockSpec(a.shape, lambda i: (0,) * a.ndim)
    acts = [x, onsa, sc_prev, lc_prev, h0]
    state = jax.ShapeDtypeStruct((n, w), F32)
    return pl.pallas_call(
        _mix_decode_kernel,
        grid=(1,),
        in_specs=[full(a) for a in acts] + [_const_spec(a.shape) for a in wargs],
        out_specs=[pl.BlockSpec((n, d), lambda i: (0, 0))] + [pl.BlockSpec((n, w), lambda i: (0, 0))] * 3,
        out_shape=[jax.ShapeDtypeStruct((n, d), F32), state, state, state],
        compiler_params=_params(("arbitrary",)),
        name="mix_decode")(*acts, *wargs)


def _ffn_kernel(*refs, final, n_chunks):
    if final:
        x_ref, g_ref, wi_ref, wout_ref, gf_ref, y_ref = refs
    else:
        x_ref, g_ref, wi_ref, wout_ref, y_ref = refs
    x = x_ref[...]
    h = _rms(x, g_ref[...]).astype(BF16)
    d_ff = wout_ref.shape[0]
    ck = d_ff // n_chunks
    y = x
    for c in range(n_chunks):
        gate = _dot(h, wi_ref[:, c * ck:(c + 1) * ck])
        up = _dot(h, wi_ref[:, d_ff + c * ck:d_ff + (c + 1) * ck])
        y = y + _dot((jax.nn.silu(gate) * up).astype(BF16), wout_ref[c * ck:(c + 1) * ck, :])
    if final:
        y = _rms(y, gf_ref[...])
    y_ref[...] = y


def _ffn(x, g, wi, wout, gf, tm):
    n, d = x.shape
    final = gf is not None
    d_ff = wout.shape[0]
    n_chunks = 2 if d_ff % (2 * LANES) == 0 else 1
    row = lambda i: (i, 0)
    args = [x, g, wi, wout] + ([gf] if final else [])
    return pl.pallas_call(
        functools.partial(_ffn_kernel, final=final, n_chunks=n_chunks),
        grid=(n // tm,),
        in_specs=[pl.BlockSpec((tm, d), row)] + [_const_spec(a.shape) for a in args[1:]],
        out_specs=pl.BlockSpec((tm, d), row),
        out_shape=jax.ShapeDtypeStruct((n, d), F32),
        compiler_params=_params(("arbitrary",)),
        name="ffn")(*args)


def _group_onehot(rows, cols, row_div, col_div):
    r = lax.broadcasted_iota(jnp.int32, (rows, cols), 0) // row_div
    c = lax.broadcasted_iota(jnp.int32, (rows, cols), 1) // col_div
    return r == c


def _tile_lanes(q, reps):
    hd = q.shape[1]
    r = lax.broadcasted_iota(jnp.int32, (hd, reps * hd), 0)
    c = lax.broadcasted_iota(jnp.int32, (hd, reps * hd), 1)
    rep = jnp.where(c % hd == r, 1.0, 0.0).astype(BF16)
    return _dot(q, rep)


def _fold_heads(o):
    own = _group_onehot(o.shape[0], o.shape[1], Q_PER_KV, HEAD_DIM)
    o = jnp.where(own, o, 0.0)
    out = o[:, :HEAD_DIM]
    for g in range(1, o.shape[1] // HEAD_DIM):
        out = out + o[:, g * HEAD_DIM:(g + 1) * HEAD_DIM]
    return out


def _dec_cmp_kernel(pt_ref, q_ref, gate_ref, wct_ref, cache_ref, oc_ref, idx_ref, buf, sem,
                    *, row0, n_pages, page, past, n_sel):
    b = pl.program_id(0)
    nb = pl.num_programs(0)

    def copies(bb, slot):
        out = []
        for pg in range(n_pages):
            base = (row0 + pt_ref[bb * n_pages + pg]) * 4
            for kind in range(2):
                out.append(pltpu.make_async_copy(cache_ref.at[base + kind], buf.at[slot, kind, pg],
                                                 sem.at[slot]))
        return out

    slot = b % 2

    @pl.when(b == 0)
    def _():
        for c in copies(0, 0):
            c.start()

    @pl.when(b + 1 < nb)
    def _():
        for c in copies(b + 1, 1 - slot):
            c.start()

    for c in copies(b, slot):
        c.wait()

    nblk = past // CMP_BLOCK
    per_page = page // CMP_BLOCK
    npad = -(-(nblk + 1) // LANES) * LANES
    lane_blk = lax.broadcasted_iota(jnp.int32, (GROUP_W, page), 1) // CMP_BLOCK
    out_lane = lax.broadcasted_iota(jnp.int32, (GROUP_W, npad), 1)
    cmp_t = []
    for kind in range(2):
        w = wct_ref[kind]
        acc = jnp.zeros((GROUP_W, npad), F32)
        for pg in range(n_pages):
            y = buf[slot, kind, pg].reshape(GROUP_W, page) * w
            for j in range(per_page):
                part = jnp.sum(jnp.where(lane_blk == j, y, 0.0), axis=1, keepdims=True)
                acc = jnp.where(out_lane == pg * per_page + j, part, acc)
        cmp_t.append(acc.astype(BF16))
    kc_t, vc_t = cmp_t

    q = q_ref[...]
    own = _group_onehot(N_HEADS, GROUP_W, Q_PER_KV, HEAD_DIM)
    qbd = jnp.where(own, _tile_lanes(q, N_KV), 0.0).astype(BF16)
    s = _dot(qbd, kc_t)
    blk = lax.broadcasted_iota(jnp.int32, (N_HEADS, npad), 1)
    valid = (blk + 1) * CMP_BLOCK - 1 <= past
    s = jnp.where(valid, s, NEG)
    m = jnp.max(s, axis=-1, keepdims=True)
    p = jnp.where(valid, jnp.exp(s - m), 0.0)
    p = p / jnp.maximum(jnp.sum(p, axis=-1, keepdims=True), TINY)
    o = _fold_heads(_dot_nt(p.astype(BF16), vc_t))
    oc_ref[...] = gate_ref[:, 0:1] * o

    blk8 = lax.broadcasted_iota(jnp.int32, (SUBLANES, npad), 1)
    row8 = lax.broadcasted_iota(jnp.int32, (SUBLANES, npad), 0)
    imp = jnp.zeros((SUBLANES, npad), F32)
    for g in range(N_KV):
        psum = jnp.sum(p[g * Q_PER_KV:(g + 1) * Q_PER_KV], axis=0, keepdims=True)
        imp = jnp.where(row8 == g, psum, imp)
    cur = past // SEL_BLOCK
    imp = imp + jnp.where(blk8 == cur, FORCE, 0.0) + jnp.where(blk8 == 0, 0.5 * FORCE, 0.0)
    imp = jnp.where(blk8 <= cur, imp, -jnp.inf)
    blkf = blk8.astype(F32)
    idx = jnp.zeros((SUBLANES, LANES), F32)
    lane = lax.broadcasted_iota(jnp.int32, (SUBLANES, LANES), 1)
    for j in range(n_sel):
        mx = jnp.max(imp, axis=-1, keepdims=True)
        first = jnp.min(jnp.where(imp == mx, blkf, float(npad)), axis=-1, keepdims=True)
        idx = jnp.where(lane == j, first, idx)
        imp = jnp.where(blkf == first, -jnp.inf, imp)
    idx_ref[...] = idx.astype(jnp.int32)


def _dec_cmp(page_table, q3, gates3, wct, cache4, row0, past):
    db = q3.shape[0]
    n_pages = page_table.shape[1]
    page = cache4.shape[3]
    n_sel = min(N_SEL, past // SEL_BLOCK + 1)
    per_b = lambda b, pt: (b, 0, 0)
    grid_spec = pltpu.PrefetchScalarGridSpec(
        num_scalar_prefetch=1,
        grid=(db,),
        in_specs=[pl.BlockSpec((None, N_HEADS, HEAD_DIM), per_b),
                  pl.BlockSpec((None, N_HEADS, LANES), per_b),
                  pl.BlockSpec(wct.shape, lambda b, pt: (0, 0, 0)),
                  pl.BlockSpec(memory_space=pl.ANY)],
        out_specs=[pl.BlockSpec((None, N_HEADS, HEAD_DIM), per_b),
                   pl.BlockSpec((None, SUBLANES, LANES), per_b)],
        scratch_shapes=[pltpu.VMEM((2, 2, n_pages, N_KV, HEAD_DIM, page), F32),
                        pltpu.SemaphoreType.DMA((2,))])
    return pl.pallas_call(
        functools.partial(_dec_cmp_kernel, row0=row0, n_pages=n_pages, page=page, past=past, n_sel=n_sel),
        grid_spec=grid_spec,
        out_shape=[jax.ShapeDtypeStruct((db, N_HEADS, HEAD_DIM), F32),
                   jax.ShapeDtypeStruct((db, SUBLANES, LANES), jnp.int32)],
        compiler_params=_params(("arbitrary",)),
        name="dec_cmp_topk")(page_table.reshape(-1), q3, gates3, wct, cache4)


def _dec_attn_kernel(idx_ref, pt_ref, q_ref, gate_ref, oc_ref, kvn_ref, kwcol_ref, win_ref, cache_ref,
                     o_ref, wout_ref, kbuf, vbuf, sem, *, row0, n_sel, past, page):
    b = pl.program_id(0)
    nblk_past = past // SEL_BLOCK
    per_page = page // SEL_BLOCK
    n_pages = nblk_past // per_page
    win_len = win_ref.shape[1]

    def block_copies(g, j, n):
        base = (row0 + pt_ref[b * n_pages + n // per_page]) * 4
        dst = pl.ds(j * page, page)
        return (pltpu.make_async_copy(cache_ref.at[base + 2, g], kbuf.at[g, :, dst], sem.at[0]),
                pltpu.make_async_copy(cache_ref.at[base + 3, g], vbuf.at[g, :, dst], sem.at[1]))

    for g in range(N_KV):
        for j in range(n_sel):
            n = idx_ref[(b * N_KV + g) * n_sel + j]

            @pl.when(n < nblk_past)
            def _():
                for c in block_copies(g, j, n):
                    c.start()

            @pl.when(n >= nblk_past)
            def _():
                zeros = jnp.zeros((HEAD_DIM, page), F32)
                kbuf[g, :, j * page:(j + 1) * page] = zeros
                vbuf[g, :, j * page:(j + 1) * page] = zeros

    q = q_ref[...]
    own = _group_onehot(N_HEADS, GROUP_W, Q_PER_KV, HEAD_DIM)
    qbd_f = jnp.where(own, _tile_lanes(q, N_KV), 0.0)
    qbd = qbd_f.astype(BF16)
    gt = gate_ref[...]
    kvn = kvn_ref[...]

    win = win_ref[...]
    s = _dot(qbd, win[:GROUP_W].astype(BF16))
    colw = lax.broadcasted_iota(jnp.int32, s.shape, 1)
    s = jnp.where(win_len - colw < WINDOW, s, NEG)
    s_new = jnp.sum(qbd_f * kvn[:, 4 * GROUP_W:5 * GROUP_W], axis=-1, keepdims=True)
    m = jnp.maximum(jnp.max(s, axis=-1, keepdims=True), s_new)
    p = jnp.exp(s - m)
    p_new = jnp.exp(s_new - m)
    l = jnp.sum(p, axis=-1, keepdims=True) + p_new
    o_w = _fold_heads(_dot_nt(p.astype(BF16), win[GROUP_W:].astype(BF16))
                      + p_new * kvn[:, 5 * GROUP_W:]) / l

    lane_b = lax.broadcasted_iota(jnp.int32, kwcol_ref.shape, 1)
    new_col = jnp.sum(jnp.where(lane_b == b, kwcol_ref[...], 0.0), axis=1, keepdims=True)
    pos = lax.broadcasted_iota(jnp.int32, win.shape, 1)
    wout_ref[...] = jnp.where(pos == win_len - 1, new_col, pltpu.roll(win, win_len - 1, axis=1))

    for g in range(N_KV):
        for j in range(n_sel):
            n = idx_ref[(b * N_KV + g) * n_sel + j]

            @pl.when(n < nblk_past)
            def _():
                for c in block_copies(g, j, n):
                    c.wait()

    nkeys = n_sel * page
    rowg = lax.broadcasted_iota(jnp.int32, (N_HEADS, HEAD_DIM), 0) // Q_PER_KV
    rows = lax.broadcasted_iota(jnp.int32, (N_HEADS, nkeys), 0) // Q_PER_KV
    lanes = lax.broadcasted_iota(jnp.int32, (N_HEADS, nkeys), 1)
    colj = lanes // page
    in_page = (lanes % page) // SEL_BLOCK
    qf = q.astype(F32)
    s = jnp.zeros((N_HEADS, nkeys), F32)
    okay = jnp.zeros((N_HEADS, nkeys), F32)
    for g in range(N_KV):
        qg = jnp.where(rowg == g, qf, 0.0).astype(BF16)
        s = s + _dot(qg, kbuf[g].astype(BF16))
        for j in range(n_sel):
            n = idx_ref[(b * N_KV + g) * n_sel + j]
            flag = jnp.where(n < nblk_past, 1.0, 0.0)
            here = jnp.where(in_page == n % per_page, flag, 0.0)
            okay = jnp.where(rows == g, jnp.where(colj == j, here, okay), okay)
    s = jnp.where(okay > 0.5, s, NEG)
    s_new = jnp.sum(qbd_f * kvn[:, 2 * GROUP_W:3 * GROUP_W], axis=-1, keepdims=True)
    m = jnp.maximum(jnp.max(s, axis=-1, keepdims=True), s_new)
    p = jnp.where(okay > 0.5, jnp.exp(s - m), 0.0)
    p_new = jnp.exp(s_new - m)
    l = jnp.sum(p, axis=-1, keepdims=True) + p_new
    o_sel = jnp.zeros((N_HEADS, HEAD_DIM), F32)
    for g in range(N_KV):
        pg = jnp.where(rows == g, p, 0.0).astype(BF16)
        o_sel = o_sel + _dot_nt(pg, vbuf[g].astype(BF16))
    o_s = (o_sel + _fold_heads(p_new * jnp.where(own, kvn[:, 3 * GROUP_W:4 * GROUP_W], 0.0))) / l

    o_ref[...] = (oc_ref[...] + gt[:, 1:2] * o_s + gt[:, 2:3] * o_w).astype(BF16)


def _dec_attn(idx, page_table, q3, gates3, oc, kv_new, kw_cols, win, cache4, row0, win_row0, past):
    db = q3.shape[0]
    page = cache4.shape[3]
    n_sel = idx.shape[-1]
    per_b = lambda b, *_: (b, 0, 0)
    win_b = lambda b, *_: (win_row0 + b, 0, 0)
    win_block = (None,) + win.shape[1:]
    grid_spec = pltpu.PrefetchScalarGridSpec(
        num_scalar_prefetch=2,
        grid=(db,),
        in_specs=[pl.BlockSpec((None, N_HEADS, HEAD_DIM), per_b),
                  pl.BlockSpec((None, N_HEADS, LANES), per_b),
                  pl.BlockSpec((None, N_HEADS, HEAD_DIM), per_b),
                  pl.BlockSpec((None, 1, kv_new.shape[-1]), per_b),
                  pl.BlockSpec(kw_cols.shape, lambda b, *_: (0, 0)),
                  pl.BlockSpec(win_block, win_b),
                  pl.BlockSpec(memory_space=pl.ANY)],
        out_specs=[pl.BlockSpec((None, N_HEADS, HEAD_DIM), per_b),
                   pl.BlockSpec(win_block, per_b)],
        scratch_shapes=[pltpu.VMEM((N_KV, HEAD_DIM, n_sel * page), F32),
                        pltpu.VMEM((N_KV, HEAD_DIM, n_sel * page), F32),
                        pltpu.SemaphoreType.DMA((2,))])
    return pl.pallas_call(
        functools.partial(_dec_attn_kernel, row0=row0, n_sel=n_sel, past=past, page=page),
        grid_spec=grid_spec,
        out_shape=[jax.ShapeDtypeStruct((db, N_HEADS, HEAD_DIM), BF16),
                   jax.ShapeDtypeStruct((db,) + win.shape[1:], F32)],
        compiler_params=_params(("arbitrary",)),
        name="dec_attn")(idx.reshape(-1), page_table.reshape(-1), q3, gates3, oc,
                         kv_new[:, None, :], kw_cols, win, cache4)


def _layer_weights(l, norm1_g, w_in, w_cmp_k, w_cmp_v, w_nsa_out, w_sc_conv, w_sc_out, w_lru_conv,
                   b_lru_conv, w_lru_gate, b_lru_gate, lru_lambda, w_lru_out, w_o, norm2_g, w_ffn_in,
                   w_ffn_out):
    d = w_in.shape[1]
    nsa = N_HEADS * HEAD_DIM
    kvc = 2 * GROUP_W
    sc_w = w_sc_out.shape[1]
    lru_w = w_lru_out.shape[1]
    wi = w_in[l]
    o = 0
    wq = wi[:, o:o + nsa]; o += nsa
    wkv = wi[:, o:o + 3 * kvc]; o += 3 * kvc
    wg = wi[:, o:o + 3 * N_HEADS]; o += 3 * N_HEADS
    wb = wi[:, o:o + 3 * sc_w + 2 * lru_w]; o += 3 * sc_w + 2 * lru_w
    wgm = wi[:, o:]
    wkc = wkv[:, :kvc].reshape(d, 2, N_KV, HEAD_DIM).transpose(0, 2, 1, 3).reshape(d, kvc)
    k_slc = wkv[:, kvc:kvc + GROUP_W].reshape(d, N_KV, 1, HEAD_DIM)
    k_win = wkv[:, 2 * kvc:2 * kvc + GROUP_W].reshape(d, N_KV, 1, HEAD_DIM)
    wkp = jnp.concatenate([k_slc, k_win], axis=2).reshape(d, 2 * GROUP_W)
    pad_g = ((0, 0), (0, 0), (0, 2 * SUBLANES - 3 * Q_PER_KV))
    wgt = jnp.pad(wg.reshape(d, N_KV, 3 * Q_PER_KV), pad_g).reshape(d, N_KV * 2 * SUBLANES).T
    pad_h = ((0, 0), (0, 0), (0, LANES - 3))
    wg_s = jnp.pad(wg.reshape(d, N_HEADS, 3), pad_h).reshape(d, N_HEADS * LANES)
    ck, cv = w_cmp_k[l], w_cmp_v[l]
    wc_p = jnp.tile(jnp.concatenate([ck, cv], axis=1), (1, N_KV))
    wct = jnp.stack([jnp.tile(ck.T, (N_KV, 2)), jnp.tile(cv.T, (N_KV, 2))])
    eye = jnp.eye(LRU_HEADS, dtype=F32)
    wgate = jnp.concatenate(
        [(eye[:, None, :, None] * w_lru_gate[l, k][:, :, None, :]).reshape(lru_w, lru_w) for k in range(2)],
        axis=1)
    bf = lambda a: a.astype(BF16)
    return {
        "g1": norm1_g[l][None], "wq": bf(wq), "wqt": bf(wq.T), "wkvt": bf(wkv.T), "wkv": bf(wkv),
        "wkc": bf(wkc), "wkp": bf(wkp), "wgt": bf(wgt), "wg_s": bf(wg_s), "wc_p": wc_p, "wct": wct,
        "wb": bf(wb), "wgm": bf(wgm), "wscc": w_sc_conv[l], "wlc": w_lru_conv[l],
        "blc": b_lru_conv[l][None], "wgate": bf(wgate), "bgate": b_lru_gate[l].reshape(1, 2 * lru_w),
        "lam": lru_lambda[l][None], "wsco": bf(w_sc_out[l]), "wlo": bf(w_lru_out[l]),
        "wno": bf(w_nsa_out[l]), "wo": bf(w_o[l]), "g2": norm2_g[l][None],
        "wfi": bf(w_ffn_in[l]), "wfo": bf(w_ffn_out[l]),
    }


def _pick_tile(n, pref):
    t = min(n, pref)
    while n % t:
        t //= 2
    return t


def kernel(x_prompt, x_sample, cache_kv, cache_win, state_sconv, state_lru_conv, state_lru_h, page_table, norm1_g, w_in, w_cmp_k, w_cmp_v, w_nsa_out, w_sc_conv, w_sc_out, w_lru_conv, b_lru_conv, w_lru_gate, b_lru_gate, lru_lambda, w_lru_out, w_o, norm2_g, w_ffn_in, w_ffn_out, final_g):
    batch, seq, d = x_prompt.shape
    db, dec_seq, _ = x_sample.shape
    depth = w_in.shape[0]
    n_pool, page = cache_kv.shape[1], cache_kv.shape[2]
    past = page_table.shape[1] * page
    win_len = cache_win.shape[2]
    assert dec_seq == 1 and past % SEL_BLOCK == 0 and page % SEL_BLOCK == 0
    assert seq % 256 == 0 and page == LANES

    tq = 256
    tm_qkv = _pick_tile(seq, 512)
    tm_mix = _pick_tile(seq, 256)
    tm_ffn = _pick_tile(batch * seq, 512)
    gf = final_g[None]

    cache4 = jnp.transpose(cache_kv, (0, 1, 3, 4, 5, 2)).reshape(depth * n_pool * 4, N_KV, HEAD_DIM, page)
    win_t = jnp.transpose(cache_win, (0, 1, 3, 4, 5, 2)).reshape(depth * db, 2 * GROUP_W, win_len)

    xp = x_prompt.reshape(batch * seq, d)
    xs = x_sample.reshape(db, d)
    outs_p, outs_s = [], []
    for l in range(depth):
        lw = _layer_weights(l, norm1_g, w_in, w_cmp_k, w_cmp_v, w_nsa_out, w_sc_conv, w_sc_out,
                            w_lru_conv, b_lru_conv, w_lru_gate, b_lru_gate, lru_lambda, w_lru_out,
                            w_o, norm2_g, w_ffn_in, w_ffn_out)
        last = l == depth - 1
        qt, kvt, kwt, gates_t, kp, vt, cmpk = _qkv_prompt(xp, lw, batch, seq, tm_qkv, tq)
        cmpt = cmpk.reshape(batch, seq // CMP_BLOCK, N_KV, LANES).transpose(0, 2, 3, 1)
        oct, selt = _cmp_topk(qt, cmpk, cmpt, gates_t, batch, seq, tq)
        onsa = _flash(qt, kp, vt, selt, gates_t, oct, batch, seq, tq, qc=LANES)
        x1, sct, lct, hl = _mix_prompt(xp, onsa, lw, batch, seq, tm_mix)
        xp = _ffn(x1, lw["g2"], lw["wfi"], lw["wfo"], gf if last else None, tm_ffn)
        w_keep = min(WINDOW, seq)
        outs_p.append((kvt, kwt[:, :, seq - w_keep:], sct[:, SUBLANES - (SC_K - 1):],
                       lct[:, SUBLANES - (LRU_K - 1):], hl[:, SUBLANES - 1]))
        q, kvt, kwt, gates, kvr = _qkv_decode(xs, lw)
        q3 = q.reshape(db, N_HEADS, HEAD_DIM)
        gates3 = gates.reshape(db, N_HEADS, LANES)
        oc, idx = _dec_cmp(page_table, q3, gates3, lw["wct"], cache4, l * n_pool, past)
        n_sel = min(N_SEL, past // SEL_BLOCK + 1)
        idx = idx[:, :N_KV, :n_sel]
        onsa, win_new = _dec_attn(idx, page_table, q3, gates3, oc, kvr, kwt, win_t, cache4,
                                  l * n_pool, l * db, past)
        sc_prev = state_sconv[l].reshape(db, -1)
        lc_prev = state_lru_conv[l].reshape(db, -1)
        x1, cx, xl, hn = _mix_decode(xs, onsa.reshape(db, N_HEADS * HEAD_DIM), sc_prev, lc_prev,
                                     state_lru_h[l], lw)
        xs = _ffn(x1, lw["g2"], lw["wfi"], lw["wfo"], gf if last else None, db)
        outs_s.append((kvt, win_new,
                       jnp.concatenate([state_sconv[l][:, 1:], cx[:, None]], axis=1),
                       jnp.concatenate([state_lru_conv[l][:, 1:], xl[:, None]], axis=1),
                       hn))
    stack = lambda outs, i: jnp.stack([o[i] for o in outs])
    rows_out = lambda a, kinds: jnp.transpose(
        a.reshape(a.shape[:2] + (kinds, N_KV, HEAD_DIM, a.shape[-1])), (0, 1, 5, 2, 3, 4))
    y_prompt = xp.reshape(batch, seq, d)
    y_sample = xs.reshape(db, 1, d)
    kv_s = jnp.transpose(stack(outs_s, 0).reshape(depth, 1, 4, N_KV, HEAD_DIM, db), (0, 5, 1, 2, 3, 4))
    return (y_prompt, y_sample,
            rows_out(stack(outs_p, 0), 4), rows_out(stack(outs_p, 1), 2),
            stack(outs_p, 2), stack(outs_p, 3), stack(outs_p, 4),
            kv_s, rows_out(stack(outs_s, 1), 2),
            stack(outs_s, 2), stack(outs_s, 3), stack(outs_s, 4))
```

```python
import functools

import jax
import jax.numpy as jnp
from jax import lax
from jax.experimental import pallas as pl
from jax.experimental.pallas import tpu as pltpu

N_HEADS = 16
N_KV = 4
HEAD_DIM = 64
Q_PER_KV = N_HEADS // N_KV
CMP_BLOCK = 64
SEL_BLOCK = CMP_BLOCK
N_SEL = 8
WINDOW = 512
SC_K = 3
LRU_K = 4
LRU_HEADS = 8
LRU_C = 8.0
SCALE = HEAD_DIM ** -0.5
LOG2E = 1.4426950408889634
EPS = 1e-6
NEG = -1e30
TINY = 1e-20
FORCE = 1e6

LANES = 128
SUBLANES = 8
VMEM_LIMIT = 56 * 1024 * 1024
GROUP_W = N_KV * HEAD_DIM
DEC_SAMPLES_PER_STEP = 4

F32 = jnp.float32
BF16 = jnp.bfloat16
_NT = (((1,), (1,)), ((), ()))


def _params(sem):
    return pltpu.CompilerParams(dimension_semantics=sem, vmem_limit_bytes=VMEM_LIMIT)


def _const_spec(shape):
    nd = len(shape)
    return pl.BlockSpec(shape, lambda *_: (0,) * nd, pipeline_mode=pl.Buffered(1))


def _rms(x, g):
    y = x * lax.rsqrt(jnp.mean(x * x, axis=-1, keepdims=True) + EPS)
    return y * g


def _dot(a, b):
    return jnp.dot(a, b, preferred_element_type=F32)


def _dot_nt(a, b):
    return lax.dot_general(a, b, _NT, preferred_element_type=F32)


def _qkv_prompt_kernel(x_ref, g_ref, wqt_ref, wkvt_ref, wgt_ref, wkp_ref, wkc_ref, wc_ref,
                       qt_ref, kvt_ref, kwt_ref, gt_ref, kp_ref, vt_ref, cmp_ref, *, tk):
    tm = x_ref.shape[0]
    h = _rms(x_ref[...], g_ref[...]).astype(BF16)
    qt_ref[...] = (_dot_nt(wqt_ref[...], h) * (SCALE * LOG2E)).astype(BF16)
    kvt = _dot_nt(wkvt_ref[...], h)
    n_rows = kvt_ref.shape[0]
    kvt_ref[...] = kvt[:n_rows]
    kwt_ref[...] = kvt[n_rows:]
    gt_ref[...] = jax.nn.sigmoid(_dot_nt(wgt_ref[...], h))
    kp_ref[...] = _dot(h, wkp_ref[...]).astype(BF16)
    v_sel = kvt[3 * GROUP_W:4 * GROUP_W]
    v_win = kvt[5 * GROUP_W:]
    for c in range(tm // tk):
        cs = slice(c * tk, (c + 1) * tk)
        vt_ref[c, :GROUP_W] = v_sel[:, cs].astype(BF16)
        vt_ref[c, GROUP_W:] = v_win[:, cs].astype(BF16)
    width = wc_ref.shape[1]
    blocks = _dot(h, wkc_ref[...]).reshape(tm // CMP_BLOCK, CMP_BLOCK, width)
    cmp_ref[...] = jnp.sum(blocks * wc_ref[...][None], axis=1)


def _qkv_prompt(x, lw, batch, seq, tm, tk):
    n, d = x.shape
    nt = seq // tm
    n_rows = 4 * GROUP_W
    n_win = 2 * GROUP_W
    nq = N_HEADS * HEAD_DIM
    row = lambda b, t: (b * nt + t, 0)
    colblk = lambda b, t: (b, 0, t)
    args = [x, lw["g1"], lw["wqt"], lw["wkvt"], lw["wgt"], lw["wkp"], lw["wkc"], lw["wc_p"]]
    n_gate = lw["wgt"].shape[0]
    n_kp = lw["wkp"].shape[1]
    n_cmp = lw["wc_p"].shape[1]
    return pl.pallas_call(
        functools.partial(_qkv_prompt_kernel, tk=tk),
        grid=(batch, nt),
        in_specs=[pl.BlockSpec((tm, d), row)] + [_const_spec(a.shape) for a in args[1:]],
        out_specs=[pl.BlockSpec((None, nq, tm), colblk),
                   pl.BlockSpec((None, n_rows, tm), colblk),
                   pl.BlockSpec((None, n_win, tm), colblk),
                   pl.BlockSpec((None, n_gate, tm), colblk),
                   pl.BlockSpec((tm, n_kp), row),
                   pl.BlockSpec((None, tm // tk, n_win, tk), lambda b, t: (b, t, 0, 0)),
                   pl.BlockSpec((tm // CMP_BLOCK, n_cmp), row)],
        out_shape=[jax.ShapeDtypeStruct((batch, nq, seq), BF16),
                   jax.ShapeDtypeStruct((batch, n_rows, seq), F32),
                   jax.ShapeDtypeStruct((batch, n_win, seq), F32),
                   jax.ShapeDtypeStruct((batch, n_gate, seq), F32),
                   jax.ShapeDtypeStruct((n, n_kp), BF16),
                   jax.ShapeDtypeStruct((batch, seq // tk, n_win, tk), BF16),
                   jax.ShapeDtypeStruct((n // CMP_BLOCK, n_cmp), F32)],
        compiler_params=_params(("arbitrary", "arbitrary")), name="qkv_prompt")(*args)


def _qkv_decode_kernel(x_ref, g_ref, wq_ref, wkvt_ref, wg_ref, wkv_ref,
                       q_ref, kvt_ref, kwt_ref, gate_ref, kvr_ref):
    h = _rms(x_ref[...], g_ref[...]).astype(BF16)
    q_ref[...] = (_dot(h, wq_ref[...]) * SCALE).astype(BF16)
    kvt = _dot_nt(wkvt_ref[...], h)
    n_rows = kvt_ref.shape[0]
    kvt_ref[...] = kvt[:n_rows]
    kwt_ref[...] = kvt[n_rows:]
    gate_ref[...] = jax.nn.sigmoid(_dot(h, wg_ref[...]))
    kvr_ref[...] = _dot(h, wkv_ref[...])


def _qkv_decode(x, lw):
    n, d = x.shape
    args = [x, lw["g1"], lw["wq"], lw["wkvt"], lw["wg_s"], lw["wkv"]]
    full = lambda shape: pl.BlockSpec(shape, lambda i: (0,) * len(shape))
    shapes = [((n, lw["wq"].shape[1]), BF16), ((4 * GROUP_W, n), F32), ((2 * GROUP_W, n), F32),
              ((n, lw["wg_s"].shape[1]), F32), ((n, lw["wkv"].shape[1]), F32)]
    return pl.pallas_call(
        _qkv_decode_kernel,
        grid=(1,),
        in_specs=[full(x.shape)] + [_const_spec(a.shape) for a in args[1:]],
        out_specs=[full(s) for s, _ in shapes],
        out_shape=[jax.ShapeDtypeStruct(s, t) for s, t in shapes],
        compiler_params=_params(("arbitrary",)), name="qkv_decode")(*args)


def _cmp_topk_kernel(qt_ref, cmp_ref, cmpt_ref, gt_ref, oct_ref, selt_ref, qpad, *, n_sel):
    tq = qt_ref.shape[1]
    nblk = cmp_ref.shape[0]
    t0 = pl.program_id(2) * tq
    kcb = cmp_ref[...].astype(BF16)
    kct = cmpt_ref[...].astype(BF16)
    blk = lax.broadcasted_iota(jnp.int32, (nblk, tq), 0)
    tpos = t0 + lax.broadcasted_iota(jnp.int32, (nblk, tq), 1)
    valid = (blk + 1) * CMP_BLOCK - 1 <= tpos
    gt = gt_ref[...]
    qpad[HEAD_DIM:, :] = jnp.zeros((HEAD_DIM, tq), BF16)
    imp = jnp.zeros((nblk, tq), F32)
    for r in range(Q_PER_KV):
        hs = slice(r * HEAD_DIM, (r + 1) * HEAD_DIM)
        qpad[:HEAD_DIM, :] = qt_ref[hs, :]
        s = _dot(kcb, qpad[...])
        s = jnp.where(valid, s, NEG)
        m = jnp.max(s, axis=0, keepdims=True)
        p = jnp.where(valid, jnp.exp2(s - m), 0.0)
        p = p / jnp.maximum(jnp.sum(p, axis=0, keepdims=True), TINY)
        imp = imp + p
        o = _dot(kct, p.astype(BF16))
        oct_ref[hs, :] = gt[3 * r:3 * r + 1, :] * o[HEAD_DIM:, :]
    cur = lax.shift_right_logical(tpos, 6)
    imp = imp + jnp.where(blk == cur, FORCE, 0.0) + jnp.where(blk == 0, 0.5 * FORCE, 0.0)
    imp = jnp.where(blk <= cur, imp, -FORCE)
    blkf = blk.astype(F32)
    sel = jnp.zeros((nblk, tq), F32)
    for _ in range(n_sel):
        mx = jnp.max(imp, axis=0, keepdims=True)
        first = jnp.min(jnp.where(imp == mx, blkf, float(nblk)), axis=0, keepdims=True)
        hit = blkf == first
        sel = jnp.where(hit, 1.0, sel)
        imp = jnp.where(hit, -jnp.inf, imp)
    selt_ref[...] = jnp.where(blk <= cur, sel, 0.0).astype(BF16)


def _cmp_topk(qt, cmpk, cmpt, gates_t, batch, seq, tq):
    nblk = seq // CMP_BLOCK
    nq = seq // tq
    gw = Q_PER_KV * HEAD_DIM
    n_gate = gates_t.shape[1] // N_KV
    tile = lambda b, g, i: (b, g, i)
    return pl.pallas_call(
        functools.partial(_cmp_topk_kernel, n_sel=min(N_SEL, nblk)),
        grid=(batch, N_KV, nq),
        in_specs=[pl.BlockSpec((None, gw, tq), tile),
                  pl.BlockSpec((nblk, LANES), lambda b, g, i: (b, g)),
                  pl.BlockSpec((None, None, LANES, nblk), lambda b, g, i: (b, g, 0, 0)),
                  pl.BlockSpec((None, n_gate, tq), tile)],
        out_specs=[pl.BlockSpec((None, gw, tq), tile),
                   pl.BlockSpec((None, None, nblk, tq), lambda b, g, i: (g, b, 0, i))],
        out_shape=[jax.ShapeDtypeStruct((batch, N_HEADS * HEAD_DIM, seq), F32),
                   jax.ShapeDtypeStruct((N_KV, batch, nblk, seq), BF16)],
        scratch_shapes=[pltpu.VMEM((LANES, tq), BF16)],
        compiler_params=_params(("arbitrary", "arbitrary", "arbitrary")),
        name="cmp_topk")(qt, cmpk, cmpt, gates_t)


def _flash_kernel(qt_ref, kp_ref, vs_ref, vw_ref, selt_ref, gt_ref, oct_ref, o_ref,
                  qs_ref, qw_ref, bias_ref, s_ref, m_ref, l_ref, acc_ref, out_ref, *, n_back, qc):
    tq = qt_ref.shape[1]
    nblk = selt_ref.shape[0]
    tk = vs_ref.shape[2]
    i = pl.program_id(2)
    t0 = i * tq
    cols = Q_PER_KV * tq
    zeros = jnp.zeros((HEAD_DIM, cols), BF16)
    qs_ref[HEAD_DIM:, :] = zeros
    qw_ref[:HEAD_DIM, :] = zeros
    for r in range(Q_PER_KV):
        qh = qt_ref[r * HEAD_DIM:(r + 1) * HEAD_DIM, :]
        qs_ref[:HEAD_DIM, r * tq:(r + 1) * tq] = qh
        qw_ref[HEAD_DIM:, r * tq:(r + 1) * tq] = qh
    key0 = lax.broadcasted_iota(jnp.int32, (tk, tq), 0)
    tpos = t0 + lax.broadcasted_iota(jnp.int32, (tk, tq), 1)
    e_blk = lax.shift_right_logical(lax.broadcasted_iota(jnp.int32, (tk, nblk), 0), 6)
    e_col = lax.broadcasted_iota(jnp.int32, (tk, nblk), 1)

    def reset():
        m_ref[...] = jnp.full(m_ref.shape, NEG, F32)
        l_ref[...] = jnp.zeros(l_ref.shape, F32)
        acc_ref[...] = jnp.zeros(acc_ref.shape, F32)

    def key_tile(kt):
        return kp_ref[pl.ds(pl.multiple_of(kt * tk, tk), tk), :]

    def fill_scores(q_ref, kt):
        kp = key_tile(kt)
        for c in range(cols // qc):
            b0 = (c * qc) % tq
            s_ref[:, c * qc:(c + 1) * qc] = _dot(kp, q_ref[:, c * qc:(c + 1) * qc]) + bias_ref[:, b0:b0 + qc]

    def attend(q_ref, v_ref, kt, kt_next):
        kp = key_tile(kt_next)
        vt = v_ref[kt]
        pending = None
        for c in range(cols // qc):
            ls = slice(c * qc, (c + 1) * qc)
            b0 = (c * qc) % tq
            s = s_ref[:, ls]
            s_ref[:, ls] = _dot(kp, q_ref[:, ls]) + bias_ref[:, b0:b0 + qc]
            m_old = m_ref[:, ls]
            m_new = jnp.maximum(m_old, jnp.max(s, axis=0, keepdims=True))
            alpha = jnp.exp2(m_old - m_new)
            p = jnp.exp2(s - m_new)
            l_ref[:, ls] = alpha * l_ref[:, ls] + jnp.sum(p, axis=0, keepdims=True)
            m_ref[:, ls] = m_new
            pv = _dot(vt, p.astype(BF16))
            if pending is not None:
                pls, palpha, ppv = pending
                acc_ref[:, pls] = palpha * acc_ref[:, pls] + ppv
            pending = (ls, alpha, pv)
        pls, palpha, ppv = pending
        acc_ref[:, pls] = palpha * acc_ref[:, pls] + ppv

    def emit(k):
        gt = gt_ref[...]
        for r in range(Q_PER_KV):
            ls = slice(r * tq, (r + 1) * tq)
            hs = slice(r * HEAD_DIM, (r + 1) * HEAD_DIM)
            scale = gt[3 * r + k:3 * r + k + 1, :] / l_ref[:, ls]
            out_ref[hs, :] = out_ref[hs, :] + scale * acc_ref[:, ls]

    out_ref[...] = oct_ref[...]

    def sel_bias(kt):
        expand = jnp.where(e_col == kt * (tk // SEL_BLOCK) + e_blk, 1.0, 0.0).astype(BF16)
        chosen = _dot(expand, selt_ref[...])
        key = kt * tk + key0
        bias_ref[...] = jnp.where(chosen > 0.5, jnp.where(key <= tpos, 0.0, NEG), NEG)

    def sel_body(kt, carry):
        kt_next = jnp.minimum(kt + 1, i)
        sel_bias(kt_next)
        attend(qs_ref, vs_ref, kt, kt_next)
        return carry

    reset()
    sel_bias(0)
    fill_scores(qs_ref, 0)
    lax.fori_loop(0, i + 1, sel_body, 0)
    emit(1)

    def win_bias(kt):
        dlt = tpos - (kt * tk + key0)
        bias_ref[...] = jnp.where(dlt >= 0, jnp.where(dlt < WINDOW, 0.0, NEG), NEG)

    n_win = jnp.minimum(i, n_back) + 1

    def win_body(j, carry):
        kt = i - j
        kt_next = i - jnp.minimum(j + 1, n_win - 1)
        win_bias(kt_next)
        attend(qw_ref, vw_ref, kt, kt_next)
        return carry

    reset()
    win_bias(i)
    fill_scores(qw_ref, i)
    lax.fori_loop(0, n_win, win_body, 0)
    emit(2)
    o_ref[...] = out_ref[...].T.astype(BF16)


def _flash(qt, kp, vt, selt, gates_t, oct, batch, seq, tq, qc):
    nblk = seq // SEL_BLOCK
    nq = seq // tq
    nkt, tk = vt.shape[1], vt.shape[3]
    assert tk == tq
    gw = Q_PER_KV * HEAD_DIM
    n_gate = gates_t.shape[1] // N_KV
    n_back = -(-(WINDOW - 1) // tk)
    tile = lambda b, g, i: (b, g, i)
    v_spec = lambda kind: pl.BlockSpec((None, nkt, HEAD_DIM, tk),
                                       lambda b, g, i: (b, 0, kind * N_KV + g, 0))
    cols = Q_PER_KV * tq
    return pl.pallas_call(
        functools.partial(_flash_kernel, n_back=n_back, qc=qc),
        grid=(batch, N_KV, nq),
        in_specs=[pl.BlockSpec((None, gw, tq), tile),
                  pl.BlockSpec((seq, LANES), lambda b, g, i: (b, g)),
                  v_spec(0), v_spec(1),
                  pl.BlockSpec((None, None, nblk, tq), lambda b, g, i: (g, b, 0, i)),
                  pl.BlockSpec((None, n_gate, tq), tile),
                  pl.BlockSpec((None, gw, tq), tile)],
        out_specs=pl.BlockSpec((tq, gw), lambda b, g, i: (b * nq + i, g)),
        out_shape=jax.ShapeDtypeStruct((batch * seq, N_HEADS * HEAD_DIM), BF16),
        scratch_shapes=[pltpu.VMEM((LANES, cols), BF16), pltpu.VMEM((LANES, cols), BF16),
                        pltpu.VMEM((tk, tq), F32), pltpu.VMEM((tk, cols), F32),
                        pltpu.VMEM((1, cols), F32), pltpu.VMEM((1, cols), F32),
                        pltpu.VMEM((HEAD_DIM, cols), F32), pltpu.VMEM((gw, tq), F32)],
        compiler_params=_params(("arbitrary", "arbitrary", "arbitrary")),
        name="flash_sel_win")(qt, kp, vt, vt, selt, gates_t, oct)


def _shift_rows(x, d, fill):
    row = lax.broadcasted_iota(jnp.int32, x.shape, 0)
    return jnp.where(row < d, fill, pltpu.roll(x, d, axis=0))


def _lagged(u, tail, lag):
    us = pltpu.roll(u, lag, axis=0)
    row = lax.broadcasted_iota(jnp.int32, tail.shape, 0)
    top = jnp.where(row < lag, pltpu.roll(tail, lag, axis=0), us[:SUBLANES])
    return jnp.concatenate([top, us[SUBLANES:]], axis=0)


def _mix_tail(x, h, onsa, v_sc, v_lru, wgm_ref, wsco_ref, wlo_ref, wno_ref, wo_ref):
    d = x.shape[1]
    u_sc = _dot(v_sc.astype(BF16), wsco_ref[...])
    u_lru = _dot(v_lru.astype(BF16), wlo_ref[...])
    u_nsa = _dot(onsa, wno_ref[...])
    mix = jax.nn.sigmoid(_dot(h, wgm_ref[:, :d])) * u_nsa
    mix = mix + jax.nn.sigmoid(_dot(h, wgm_ref[:, d:2 * d])) * u_sc
    mix = mix + jax.nn.sigmoid(_dot(h, wgm_ref[:, 2 * d:])) * u_lru
    return x + _dot(mix.astype(BF16), wo_ref[...])


def _lru_coeffs(xc, wgate_ref, bgate_ref, lam_ref):
    w = xc.shape[1]
    gates = jax.nn.sigmoid(_dot(xc.astype(BF16), wgate_ref[...]) + bgate_ref[...])
    lam = -lam_ref[...]
    softplus = jnp.maximum(lam, 0.0) + jnp.log1p(jnp.exp(-jnp.abs(lam)))
    log_a = -LRU_C * gates[:, :w] * softplus
    a = jnp.exp(log_a)
    th = jnp.tanh(log_a)
    mult = jnp.sqrt(-2.0 * th / (1.0 - th))
    return a, mult, gates[:, w:]


def _mix_prompt_kernel(x_ref, onsa_ref, g_ref, wb_ref, wgm_ref, wscc_ref, wlc_ref, blc_ref,
                       wgate_ref, bgate_ref, lam_ref, wsco_ref, wlo_ref, wno_ref, wo_ref,
                       x1_ref, sct_ref, lct_ref, hl_ref, sc_tail, lc_tail, h_prev):
    t = pl.program_id(1)
    tm = x_ref.shape[0]
    w = sc_tail.shape[1]

    @pl.when(t == 0)
    def _():
        sc_tail[...] = jnp.zeros_like(sc_tail)
        lc_tail[...] = jnp.zeros_like(lc_tail)
        h_prev[...] = jnp.zeros_like(h_prev)

    x = x_ref[...]
    h = _rms(x, g_ref[...]).astype(BF16)
    proj = _dot(h, wb_ref[...])
    b_sc, c_sc, x_sc = proj[:, :w], proj[:, w:2 * w], proj[:, 2 * w:3 * w]
    x_lru, y_lru = proj[:, 3 * w:4 * w], proj[:, 4 * w:]

    cx = c_sc * x_sc
    wscc = wscc_ref[...]
    tail = sc_tail[...]
    conv = cx * wscc[SC_K - 1:SC_K]
    for lag in range(1, SC_K):
        conv = conv + _lagged(cx, tail, lag) * wscc[SC_K - 1 - lag:SC_K - lag]
    sc_tail[...] = cx[tm - SUBLANES:]
    sct_ref[...] = cx[tm - SUBLANES:]
    v_sc = b_sc * conv

    wlc = wlc_ref[...]
    tail = lc_tail[...]
    xc = x_lru * wlc[LRU_K - 1:LRU_K]
    for lag in range(1, LRU_K):
        xc = xc + _lagged(x_lru, tail, lag) * wlc[LRU_K - 1 - lag:LRU_K - lag]
    xc = xc + blc_ref[...]
    lc_tail[...] = x_lru[tm - SUBLANES:]
    lct_ref[...] = x_lru[tm - SUBLANES:]

    a, mult, gate_i = _lru_coeffs(xc, wgate_ref, bgate_ref, lam_ref)
    reset = (t * tm + lax.broadcasted_iota(jnp.int32, a.shape, 0)) == 0
    a = jnp.where(reset, 0.0, a)
    mult = jnp.where(reset, 1.0, mult)
    bx = mult * gate_i * xc
    d = 1
    while d < tm:
        bx = a * _shift_rows(bx, d, 0.0) + bx
        a = a * _shift_rows(a, d, 1.0)
        d *= 2
    hs = a * h_prev[...] + bx
    h_prev[...] = hs[tm - 1:]
    hl_ref[...] = hs[tm - SUBLANES:]
    v_lru = hs * jax.nn.gelu(y_lru)

    x1_ref[...] = _mix_tail(x, h, onsa_ref[...], v_sc, v_lru, wgm_ref, wsco_ref, wlo_ref, wno_ref, wo_ref)


def _mix_decode_kernel(x_ref, onsa_ref, scp_ref, lcp_ref, h0_ref, g_ref, wb_ref, wgm_ref, wscc_ref,
                       wlc_ref, blc_ref, wgate_ref, bgate_ref, lam_ref, wsco_ref, wlo_ref, wno_ref,
                       wo_ref, x1_ref, cx_ref, xl_ref, hn_ref):
    w = h0_ref.shape[1]
    x = x_ref[...]
    h = _rms(x, g_ref[...]).astype(BF16)
    proj = _dot(h, wb_ref[...])
    b_sc, c_sc, x_sc = proj[:, :w], proj[:, w:2 * w], proj[:, 2 * w:3 * w]
    x_lru, y_lru = proj[:, 3 * w:4 * w], proj[:, 4 * w:]

    cx = c_sc * x_sc
    wscc = wscc_ref[...]
    conv = cx * wscc[SC_K - 1:SC_K]
    for k in range(SC_K - 1):
        conv = conv + scp_ref[:, k * w:(k + 1) * w] * wscc[k:k + 1]
    cx_ref[...] = cx
    v_sc = b_sc * conv

    wlc = wlc_ref[...]
    xc = x_lru * wlc[LRU_K - 1:LRU_K]
    for k in range(LRU_K - 1):
        xc = xc + lcp_ref[:, k * w:(k + 1) * w] * wlc[k:k + 1]
    xc = xc + blc_ref[...]
    xl_ref[...] = x_lru

    a, mult, gate_i = _lru_coeffs(xc, wgate_ref, bgate_ref, lam_ref)
    hs = a * h0_ref[...] + mult * gate_i * xc
    hn_ref[...] = hs
    v_lru = hs * jax.nn.gelu(y_lru)

    x1_ref[...] = _mix_tail(x, h, onsa_ref[...], v_sc, v_lru, wgm_ref, wsco_ref, wlo_ref, wno_ref, wo_ref)


def _mix_weight_args(lw):
    return [lw["g1"], lw["wb"], lw["wgm"], lw["wscc"], lw["wlc"], lw["blc"], lw["wgate"], lw["bgate"],
            lw["lam"], lw["wsco"], lw["wlo"], lw["wno"], lw["wo"]]


def _mix_prompt(x, onsa, lw, batch, seq, tm):
    n, d = x.shape
    w = lw["wsco"].shape[0]
    nt = seq // tm
    row = lambda b, t: (b * nt + t, 0)
    per_b = lambda b, t: (b, 0, 0)
    wargs = _mix_weight_args(lw)
    tail_shape = jax.ShapeDtypeStruct((batch, SUBLANES, w), F32)
    tail_spec = pl.BlockSpec((None, SUBLANES, w), per_b)
    return pl.pallas_call(
        _mix_prompt_kernel,
        grid=(batch, nt),
        in_specs=[pl.BlockSpec((tm, d), row), pl.BlockSpec((tm, onsa.shape[1]), row)]
        + [_const_spec(a.shape) for a in wargs],
        out_specs=[pl.BlockSpec((tm, d), row), tail_spec, tail_spec, tail_spec],
        out_shape=[jax.ShapeDtypeStruct((n, d), F32), tail_shape, tail_shape, tail_shape],
        scratch_shapes=[pltpu.VMEM((SUBLANES, w), F32), pltpu.VMEM((SUBLANES, w), F32),
                        pltpu.VMEM((1, w), F32)],
        compiler_params=_params(("arbitrary", "arbitrary")),
        name="mix_prompt")(x, onsa, *wargs)


def _mix_decode(x, onsa, sc_prev, lc_prev, h0, lw):
    n, d = x.shape
    w = h0.shape[1]
    wargs = _mix_weight_args(lw)
    full = lambda a: pl.BlockSpec(a.shape, lambda i: (0,) * a.ndim)
    acts = [x, onsa, sc_prev, lc_prev, h0]
    state = jax.ShapeDtypeStruct((n, w), F32)
    return pl.pallas_call(
        _mix_decode_kernel,
        grid=(1,),
        in_specs=[full(a) for a in acts] + [_const_spec(a.shape) for a in wargs],
        out_specs=[pl.BlockSpec((n, d), lambda i: (0, 0))] + [pl.BlockSpec((n, w), lambda i: (0, 0))] * 3,
        out_shape=[jax.ShapeDtypeStruct((n, d), F32), state, state, state],
        compiler_params=_params(("arbitrary",)),
        name="mix_decode")(*acts, *wargs)


def _ffn_kernel(*refs, final, n_chunks):
    if final:
        x_ref, g_ref, wi_ref, wout_ref, gf_ref, y_ref = refs
    else:
        x_ref, g_ref, wi_ref, wout_ref, y_ref = refs
    x = x_ref[...]
    h = _rms(x, g_ref[...]).astype(BF16)
    d_ff = wout_ref.shape[0]
    ck = d_ff // n_chunks
    y = x
    for c in range(n_chunks):
        gate = _dot(h, wi_ref[:, c * ck:(c + 1) * ck])
        up = _dot(h, wi_ref[:, d_ff + c * ck:d_ff + (c + 1) * ck])
        y = y + _dot((jax.nn.silu(gate) * up).astype(BF16), wout_ref[c * ck:(c + 1) * ck, :])
    if final:
        y = _rms(y, gf_ref[...])
    y_ref[...] = y


def _ffn(x, g, wi, wout, gf, tm):
    n, d = x.shape
    final = gf is not None
    d_ff = wout.shape[0]
    n_chunks = 2 if d_ff % (2 * LANES) == 0 else 1
    row = lambda i: (i, 0)
    args = [x, g, wi, wout] + ([gf] if final else [])
    return pl.pallas_call(
        functools.partial(_ffn_kernel, final=final, n_chunks=n_chunks),
        grid=(n // tm,),
        in_specs=[pl.BlockSpec((tm, d), row)] + [_const_spec(a.shape) for a in args[1:]],
        out_specs=pl.BlockSpec((tm, d), row),
        out_shape=jax.ShapeDtypeStruct((n, d), F32),
        compiler_params=_params(("arbitrary",)),
        name="ffn")(*args)


def _group_onehot(rows, cols, row_div, col_div):
    r = lax.broadcasted_iota(jnp.int32, (rows, cols), 0) // row_div
    c = lax.broadcasted_iota(jnp.int32, (rows, cols), 1) // col_div
    return r == c


def _tile_lanes(q, reps):
    hd = q.shape[1]
    r = lax.broadcasted_iota(jnp.int32, (hd, reps * hd), 0)
    c = lax.broadcasted_iota(jnp.int32, (hd, reps * hd), 1)
    rep = jnp.where(c % hd == r, 1.0, 0.0).astype(BF16)
    return _dot(q, rep)


def _fold_heads(o):
    own = _group_onehot(o.shape[0], o.shape[1], Q_PER_KV, HEAD_DIM)
    o = jnp.where(own, o, 0.0)
    out = o[:, :HEAD_DIM]
    for g in range(1, o.shape[1] // HEAD_DIM):
        out = out + o[:, g * HEAD_DIM:(g + 1) * HEAD_DIM]
    return out


def _dec_cmp_kernel(pt_ref, q_ref, gate_ref, wct_ref, seg_ref, cache_ref, oc_ref, idx_ref, buf, sem,
                    *, row0, n_pages, page, past, n_sel):
    step = pl.program_id(0)
    spb = q_ref.shape[0]

    def copies(st, slot):
        out = []
        for u in range(spb):
            for pg in range(n_pages):
                base = (row0 + pt_ref[(st * spb + u) * n_pages + pg]) * 4
                for kind in range(2):
                    out.append(pltpu.make_async_copy(cache_ref.at[base + kind],
                                                     buf.at[slot, u, kind, :, :, pl.ds(pg * page, page)],
                                                     sem.at[slot]))
        return out

    slot = step % 2

    @pl.when(step == 0)
    def _():
        for c in copies(0, 0):
            c.start()

    @pl.when(step + 1 < pl.num_programs(0))
    def _():
        for c in copies(step + 1, 1 - slot):
            c.start()

    for c in copies(step, slot):
        c.wait()

    for u in range(spb):
        _dec_cmp_sample(q_ref.at[u], gate_ref.at[u], wct_ref, seg_ref, oc_ref.at[u], idx_ref.at[u],
                        buf.at[slot, u], n_pages=n_pages, page=page, past=past, n_sel=n_sel)


def _dec_cmp_sample(q_ref, gate_ref, wct_ref, seg_ref, oc_ref, idx_ref, buf, *, n_pages, page, past, n_sel):
    npad = seg_ref.shape[1]
    kchunk = 4 * page
    cmp_t = []
    for kind in range(2):
        w = jnp.concatenate([wct_ref[kind]] * (kchunk // page), axis=1)
        acc = jnp.zeros((GROUP_W, npad), F32)
        for c in range(past // kchunk):
            cs = slice(c * kchunk, (c + 1) * kchunk)
            y = buf[kind, :, :, cs].reshape(GROUP_W, kchunk) * w
            hi = y.astype(BF16)
            lo = (y - hi.astype(F32)).astype(BF16)
            seg = seg_ref[cs, :]
            acc = acc + _dot(hi, seg) + _dot(lo, seg)
        cmp_t.append(acc.astype(BF16))
    kc_t, vc_t = cmp_t

    q = q_ref[...]
    own = _group_onehot(N_HEADS, GROUP_W, Q_PER_KV, HEAD_DIM)
    qbd = jnp.where(own, _tile_lanes(q, N_KV), 0.0).astype(BF16)
    s = _dot(qbd, kc_t)
    blk = lax.broadcasted_iota(jnp.int32, (N_HEADS, npad), 1)
    valid = (blk + 1) * CMP_BLOCK - 1 <= past
    s = jnp.where(valid, s, NEG)
    m = jnp.max(s, axis=-1, keepdims=True)
    p = jnp.where(valid, jnp.exp(s - m), 0.0)
    p = p / jnp.maximum(jnp.sum(p, axis=-1, keepdims=True), TINY)
    o = _fold_heads(_dot_nt(p.astype(BF16), vc_t))
    oc_ref[...] = gate_ref[:, 0:1] * o

    blk8 = lax.broadcasted_iota(jnp.int32, (SUBLANES, npad), 1)
    row8 = lax.broadcasted_iota(jnp.int32, (SUBLANES, npad), 0)
    imp = jnp.zeros((SUBLANES, npad), F32)
    for g in range(N_KV):
        psum = jnp.sum(p[g * Q_PER_KV:(g + 1) * Q_PER_KV], axis=0, keepdims=True)
        imp = jnp.where(row8 == g, psum, imp)
    cur = past // SEL_BLOCK
    imp = imp + jnp.where(blk8 == cur, FORCE, 0.0) + jnp.where(blk8 == 0, 0.5 * FORCE, 0.0)
    imp = jnp.where(blk8 <= cur, imp, -jnp.inf)
    blkf = blk8.astype(F32)
    idx = jnp.zeros((SUBLANES, LANES), F32)
    lane = lax.broadcasted_iota(jnp.int32, (SUBLANES, LANES), 1)
    for j in range(n_sel):
        mx = jnp.max(imp, axis=-1, keepdims=True)
        first = jnp.min(jnp.where(imp == mx, blkf, float(npad)), axis=-1, keepdims=True)
        idx = jnp.where(lane == j, first, idx)
        imp = jnp.where(blkf == first, -jnp.inf, imp)
    idx_ref[...] = idx.astype(jnp.int32)


def _dec_cmp(page_table, q3, gates3, wct, cache4, row0, past):
    db = q3.shape[0]
    n_pages = page_table.shape[1]
    page = cache4.shape[3]
    n_sel = min(N_SEL, past // SEL_BLOCK + 1)
    npad = -(-(past // CMP_BLOCK + 1) // LANES) * LANES
    seg = (jnp.arange(past)[:, None] // CMP_BLOCK == jnp.arange(npad)[None, :]).astype(BF16)
    spb = DEC_SAMPLES_PER_STEP if db % DEC_SAMPLES_PER_STEP == 0 else 1
    per_b = lambda b, pt: (b, 0, 0)
    grid_spec = pltpu.PrefetchScalarGridSpec(
        num_scalar_prefetch=1,
        grid=(db // spb,),
        in_specs=[pl.BlockSpec((spb, N_HEADS, HEAD_DIM), per_b),
                  pl.BlockSpec((spb, N_HEADS, LANES), per_b),
                  pl.BlockSpec(wct.shape, lambda b, pt: (0, 0, 0)),
                  pl.BlockSpec(seg.shape, lambda b, pt: (0, 0)),
                  pl.BlockSpec(memory_space=pl.ANY)],
        out_specs=[pl.BlockSpec((spb, N_HEADS, HEAD_DIM), per_b),
                   pl.BlockSpec((spb, SUBLANES, LANES), per_b)],
        scratch_shapes=[pltpu.VMEM((2, spb, 2, N_KV, HEAD_DIM, n_pages * page), F32),
                        pltpu.SemaphoreType.DMA((2,))])
    return pl.pallas_call(
        functools.partial(_dec_cmp_kernel, row0=row0, n_pages=n_pages, page=page, past=past, n_sel=n_sel),
        grid_spec=grid_spec,
        out_shape=[jax.ShapeDtypeStruct((db, N_HEADS, HEAD_DIM), F32),
                   jax.ShapeDtypeStruct((db, SUBLANES, LANES), jnp.int32)],
        compiler_params=_params(("arbitrary",)),
        name="dec_cmp_topk")(page_table.reshape(-1), q3, gates3, wct, seg, cache4)


def _dec_attn_kernel(idx_ref, pt_ref, q_ref, gate_ref, oc_ref, kvn_ref, kwcol_ref, win_ref, cache_ref,
                     o_ref, wout_ref, kvbuf, sem, *, row0, n_sel, past, page):
    step = pl.program_id(0)
    spb = q_ref.shape[0]
    nblk_past = past // SEL_BLOCK
    per_page = page // SEL_BLOCK
    n_pages = nblk_past // per_page

    def block_copies(bb, slot, u, g, j, n):
        base = (row0 + pt_ref[bb * n_pages + n // per_page]) * 4
        dst = pl.ds(j * page, page)
        return pltpu.make_async_copy(cache_ref.at[pl.ds(base + 2, 2), g], kvbuf.at[slot, u, g, :, :, dst],
                                     sem.at[slot])

    def gather(st, slot, wait):
        for u in range(spb):
            bb = st * spb + u
            for g in range(N_KV):
                for j in range(n_sel):
                    n = jnp.minimum(idx_ref[(bb * N_KV + g) * n_sel + j], nblk_past - 1)
                    c = block_copies(bb, slot, u, g, j, n)
                    c.wait() if wait else c.start()

    slot = step % 2

    @pl.when(step == 0)
    def _():
        gather(0, 0, wait=False)

    @pl.when(step + 1 < pl.num_programs(0))
    def _():
        gather(step + 1, 1 - slot, wait=False)

    gather(step, slot, wait=True)
    for u in range(spb):
        _dec_attn_sample(step * spb + u, idx_ref, q_ref.at[u], gate_ref.at[u], oc_ref.at[u], kvn_ref.at[u],
                         kwcol_ref, win_ref.at[u], o_ref.at[u], wout_ref.at[u], kvbuf.at[slot, u],
                         n_sel=n_sel, past=past, page=page)


def _dec_attn_sample(b, idx_ref, q_ref, gate_ref, oc_ref, kvn_ref, kwcol_ref, win_ref, o_ref, wout_ref,
                     kvbuf, *, n_sel, past, page):
    nblk_past = past // SEL_BLOCK
    per_page = page // SEL_BLOCK
    win_len = win_ref.shape[1]
    q = q_ref[...]
    own = _group_onehot(N_HEADS, GROUP_W, Q_PER_KV, HEAD_DIM)
    qbd_f = jnp.where(own, _tile_lanes(q, N_KV), 0.0)
    qbd = qbd_f.astype(BF16)
    gt = gate_ref[...]
    kvn = kvn_ref[...]

    win = win_ref[...]
    s = _dot(qbd, win[:GROUP_W].astype(BF16))
    colw = lax.broadcasted_iota(jnp.int32, s.shape, 1)
    s = jnp.where(win_len - colw < WINDOW, s, NEG)
    s_new = jnp.sum(qbd_f * kvn[:, 4 * GROUP_W:5 * GROUP_W], axis=-1, keepdims=True)
    m = jnp.maximum(jnp.max(s, axis=-1, keepdims=True), s_new)
    p = jnp.exp(s - m)
    p_new = jnp.exp(s_new - m)
    l = jnp.sum(p, axis=-1, keepdims=True) + p_new
    o_w = _fold_heads(_dot_nt(p.astype(BF16), win[GROUP_W:].astype(BF16))
                      + p_new * kvn[:, 5 * GROUP_W:]) / l

    lane_b = lax.broadcasted_iota(jnp.int32, kwcol_ref.shape, 1)
    new_col = jnp.sum(jnp.where(lane_b == b, kwcol_ref[...], 0.0), axis=1, keepdims=True)
    pos = lax.broadcasted_iota(jnp.int32, win.shape, 1)
    wout_ref[...] = jnp.where(pos == win_len - 1, new_col, pltpu.roll(win, win_len - 1, axis=1))

    nkeys = n_sel * page
    rowg = lax.broadcasted_iota(jnp.int32, (N_HEADS, HEAD_DIM), 0) // Q_PER_KV
    rows = lax.broadcasted_iota(jnp.int32, (N_HEADS, nkeys), 0) // Q_PER_KV
    lanes = lax.broadcasted_iota(jnp.int32, (N_HEADS, nkeys), 1)
    colj = lanes // page
    in_page = (lanes % page) // SEL_BLOCK
    qf = q.astype(F32)
    s = jnp.zeros((N_HEADS, nkeys), F32)
    okay = jnp.zeros((N_HEADS, nkeys), F32)
    for g in range(N_KV):
        qg = jnp.where(rowg == g, qf, 0.0).astype(BF16)
        s = s + _dot(qg, kvbuf[g, 0].astype(BF16))
        for j in range(n_sel):
            n = idx_ref[(b * N_KV + g) * n_sel + j]
            flag = jnp.where(n < nblk_past, 1.0, 0.0)
            here = jnp.where(in_page == n % per_page, flag, 0.0)
            okay = jnp.where(rows == g, jnp.where(colj == j, here, okay), okay)
    s = jnp.where(okay > 0.5, s, NEG)
    s_new = jnp.sum(qbd_f * kvn[:, 2 * GROUP_W:3 * GROUP_W], axis=-1, keepdims=True)
    m = jnp.maximum(jnp.max(s, axis=-1, keepdims=True), s_new)
    p = jnp.where(okay > 0.5, jnp.exp(s - m), 0.0)
    p_new = jnp.exp(s_new - m)
    l = jnp.sum(p, axis=-1, keepdims=True) + p_new
    o_sel = jnp.zeros((N_HEADS, HEAD_DIM), F32)
    for g in range(N_KV):
        pg = jnp.where(rows == g, p, 0.0).astype(BF16)
        o_sel = o_sel + _dot_nt(pg, kvbuf[g, 1].astype(BF16))
    o_s = (o_sel + _fold_heads(p_new * jnp.where(own, kvn[:, 3 * GROUP_W:4 * GROUP_W], 0.0))) / l

    o_ref[...] = (oc_ref[...] + gt[:, 1:2] * o_s + gt[:, 2:3] * o_w).astype(BF16)


def _dec_attn(idx, page_table, q3, gates3, oc, kv_new, kw_cols, win, cache4, row0, win_row0, past):
    db = q3.shape[0]
    page = cache4.shape[3]
    n_sel = idx.shape[-1]
    spb = DEC_SAMPLES_PER_STEP if db % DEC_SAMPLES_PER_STEP == 0 else 1
    per_b = lambda b, *_: (b, 0, 0)
    win_b = lambda b, *_: (win_row0 // spb + b, 0, 0)
    win_block = (spb,) + win.shape[1:]
    grid_spec = pltpu.PrefetchScalarGridSpec(
        num_scalar_prefetch=2,
        grid=(db // spb,),
        in_specs=[pl.BlockSpec((spb, N_HEADS, HEAD_DIM), per_b),
                  pl.BlockSpec((spb, N_HEADS, LANES), per_b),
                  pl.BlockSpec((spb, N_HEADS, HEAD_DIM), per_b),
                  pl.BlockSpec((spb, 1, kv_new.shape[-1]), per_b),
                  pl.BlockSpec(kw_cols.shape, lambda b, *_: (0, 0)),
                  pl.BlockSpec(win_block, win_b),
                  pl.BlockSpec(memory_space=pl.ANY)],
        out_specs=[pl.BlockSpec((spb, N_HEADS, HEAD_DIM), per_b),
                   pl.BlockSpec(win_block, per_b)],
        scratch_shapes=[pltpu.VMEM((2, spb, N_KV, 2, HEAD_DIM, n_sel * page), F32),
                        pltpu.SemaphoreType.DMA((2,))])
    return pl.pallas_call(
        functools.partial(_dec_attn_kernel, row0=row0, n_sel=n_sel, past=past, page=page),
        grid_spec=grid_spec,
        out_shape=[jax.ShapeDtypeStruct((db, N_HEADS, HEAD_DIM), BF16),
                   jax.ShapeDtypeStruct((db,) + win.shape[1:], F32)],
        compiler_params=_params(("arbitrary",)),
        name="dec_attn")(idx.reshape(-1), page_table.reshape(-1), q3, gates3, oc,
                         kv_new[:, None, :], kw_cols, win, cache4)


def _layer_weights(l, norm1_g, w_in, w_cmp_k, w_cmp_v, w_nsa_out, w_sc_conv, w_sc_out, w_lru_conv,
                   b_lru_conv, w_lru_gate, b_lru_gate, lru_lambda, w_lru_out, w_o, norm2_g, w_ffn_in,
                   w_ffn_out):
    d = w_in.shape[1]
    nsa = N_HEADS * HEAD_DIM
    kvc = 2 * GROUP_W
    sc_w = w_sc_out.shape[1]
    lru_w = w_lru_out.shape[1]
    wi = w_in[l]
    o = 0
    wq = wi[:, o:o + nsa]; o += nsa
    wkv = wi[:, o:o + 3 * kvc]; o += 3 * kvc
    wg = wi[:, o:o + 3 * N_HEADS]; o += 3 * N_HEADS
    wb = wi[:, o:o + 3 * sc_w + 2 * lru_w]; o += 3 * sc_w + 2 * lru_w
    wgm = wi[:, o:]
    wkc = wkv[:, :kvc].reshape(d, 2, N_KV, HEAD_DIM).transpose(0, 2, 1, 3).reshape(d, kvc)
    k_slc = wkv[:, kvc:kvc + GROUP_W].reshape(d, N_KV, 1, HEAD_DIM)
    k_win = wkv[:, 2 * kvc:2 * kvc + GROUP_W].reshape(d, N_KV, 1, HEAD_DIM)
    wkp = jnp.concatenate([k_slc, k_win], axis=2).reshape(d, 2 * GROUP_W)
    pad_g = ((0, 0), (0, 0), (0, 2 * SUBLANES - 3 * Q_PER_KV))
    wgt = jnp.pad(wg.reshape(d, N_KV, 3 * Q_PER_KV), pad_g).reshape(d, N_KV * 2 * SUBLANES).T
    pad_h = ((0, 0), (0, 0), (0, LANES - 3))
    wg_s = jnp.pad(wg.reshape(d, N_HEADS, 3), pad_h).reshape(d, N_HEADS * LANES)
    ck, cv = w_cmp_k[l], w_cmp_v[l]
    wc_p = jnp.tile(jnp.concatenate([ck, cv], axis=1), (1, N_KV))
    wct = jnp.stack([jnp.tile(ck.T, (N_KV, 2)), jnp.tile(cv.T, (N_KV, 2))])
    eye = jnp.eye(LRU_HEADS, dtype=F32)
    wgate = jnp.concatenate(
        [(eye[:, None, :, None] * w_lru_gate[l, k][:, :, None, :]).reshape(lru_w, lru_w) for k in range(2)],
        axis=1)
    bf = lambda a: a.astype(BF16)
    return {
        "g1": norm1_g[l][None], "wq": bf(wq), "wqt": bf(wq.T), "wkvt": bf(wkv.T), "wkv": bf(wkv),
        "wkc": bf(wkc), "wkp": bf(wkp), "wgt": bf(wgt), "wg_s": bf(wg_s), "wc_p": wc_p, "wct": wct,
        "wb": bf(wb), "wgm": bf(wgm), "wscc": w_sc_conv[l], "wlc": w_lru_conv[l],
        "blc": b_lru_conv[l][None], "wgate": bf(wgate), "bgate": b_lru_gate[l].reshape(1, 2 * lru_w),
        "lam": lru_lambda[l][None], "wsco": bf(w_sc_out[l]), "wlo": bf(w_lru_out[l]),
        "wno": bf(w_nsa_out[l]), "wo": bf(w_o[l]), "g2": norm2_g[l][None],
        "wfi": bf(w_ffn_in[l]), "wfo": bf(w_ffn_out[l]),
    }


def _pick_tile(n, pref):
    t = min(n, pref)
    while n % t:
        t //= 2
    return t


def kernel(x_prompt, x_sample, cache_kv, cache_win, state_sconv, state_lru_conv, state_lru_h, page_table, norm1_g, w_in, w_cmp_k, w_cmp_v, w_nsa_out, w_sc_conv, w_sc_out, w_lru_conv, b_lru_conv, w_lru_gate, b_lru_gate, lru_lambda, w_lru_out, w_o, norm2_g, w_ffn_in, w_ffn_out, final_g):
    batch, seq, d = x_prompt.shape
    db, dec_seq, _ = x_sample.shape
    depth = w_in.shape[0]
    n_pool, page = cache_kv.shape[1], cache_kv.shape[2]
    past = page_table.shape[1] * page
    win_len = cache_win.shape[2]
    assert dec_seq == 1 and past % SEL_BLOCK == 0 and page % SEL_BLOCK == 0
    assert seq % 256 == 0 and page == LANES

    tq = 256
    tm_qkv = _pick_tile(seq, 512)
    tm_mix = _pick_tile(seq, 256)
    tm_ffn = _pick_tile(batch * seq, 512)
    gf = final_g[None]

    cache4 = jnp.transpose(cache_kv, (0, 1, 3, 4, 5, 2)).reshape(depth * n_pool * 4, N_KV, HEAD_DIM, page)
    win_t = jnp.transpose(cache_win, (0, 1, 3, 4, 5, 2)).reshape(depth * db, 2 * GROUP_W, win_len)

    xp = x_prompt.reshape(batch * seq, d)
    xs = x_sample.reshape(db, d)
    outs_p, outs_s = [], []
    for l in range(depth):
        lw = _layer_weights(l, norm1_g, w_in, w_cmp_k, w_cmp_v, w_nsa_out, w_sc_conv, w_sc_out,
                            w_lru_conv, b_lru_conv, w_lru_gate, b_lru_gate, lru_lambda, w_lru_out,
                            w_o, norm2_g, w_ffn_in, w_ffn_out)
        last = l == depth - 1
        qt, kvt, kwt, gates_t, kp, vt, cmpk = _qkv_prompt(xp, lw, batch, seq, tm_qkv, tq)
        cmpt = cmpk.reshape(batch, seq // CMP_BLOCK, N_KV, LANES).transpose(0, 2, 3, 1)
        oct, selt = _cmp_topk(qt, cmpk, cmpt, gates_t, batch, seq, tq)
        onsa = _flash(qt, kp, vt, selt, gates_t, oct, batch, seq, tq, qc=LANES)
        x1, sct, lct, hl = _mix_prompt(xp, onsa, lw, batch, seq, tm_mix)
        xp = _ffn(x1, lw["g2"], lw["wfi"], lw["wfo"], gf if last else None, tm_ffn)
        w_keep = min(WINDOW, seq)
        outs_p.append((kvt, kwt[:, :, seq - w_keep:], sct[:, SUBLANES - (SC_K - 1):],
                       lct[:, SUBLANES - (LRU_K - 1):], hl[:, SUBLANES - 1]))
        q, kvt, kwt, gates, kvr = _qkv_decode(xs, lw)
        q3 = q.reshape(db, N_HEADS, HEAD_DIM)
        gates3 = gates.reshape(db, N_HEADS, LANES)
        oc, idx = _dec_cmp(page_table, q3, gates3, lw["wct"], cache4, l * n_pool, past)
        n_sel = min(N_SEL, past // SEL_BLOCK + 1)
        idx = idx[:, :N_KV, :n_sel]
        onsa, win_new = _dec_attn(idx, page_table, q3, gates3, oc, kvr, kwt, win_t, cache4,
                                  l * n_pool, l * db, past)
        sc_prev = state_sconv[l].reshape(db, -1)
        lc_prev = state_lru_conv[l].reshape(db, -1)
        x1, cx, xl, hn = _mix_decode(xs, onsa.reshape(db, N_HEADS * HEAD_DIM), sc_prev, lc_prev,
                                     state_lru_h[l], lw)
        xs = _ffn(x1, lw["g2"], lw["wfi"], lw["wfo"], gf if last else None, db)
        outs_s.append((kvt, win_new,
                       jnp.concatenate([state_sconv[l][:, 1:], cx[:, None]], axis=1),
                       jnp.concatenate([state_lru_conv[l][:, 1:], xl[:, None]], axis=1),
                       hn))
    stack = lambda outs, i: jnp.stack([o[i] for o in outs])
    rows_out = lambda a, kinds: jnp.transpose(
        a.reshape(a.shape[:2] + (kinds, N_KV, HEAD_DIM, a.shape[-1])), (0, 1, 5, 2, 3, 4))
    y_prompt = xp.reshape(batch, seq, d)
    y_sample = xs.reshape(db, 1, d)
    kv_s = jnp.transpose(stack(outs_s, 0).reshape(depth, 1, 4, N_KV, HEAD_DIM, db), (0, 5, 1, 2, 3, 4))
    return (y_prompt, y_sample,
            rows_out(stack(outs_p, 0), 4), rows_out(stack(outs_p, 1), 2),
            stack(outs_p, 2), stack(outs_p, 3), stack(outs_p, 4),
            kv_s, rows_out(stack(outs_s, 1), 2),
            stack(outs_s, 2), stack(outs_s, 3), stack(outs_s, 4))
```

```python
import functools

import jax
import jax.numpy as jnp
from jax import lax
from jax.experimental import pallas as pl
from jax.experimental.pallas import tpu as pltpu

N_HEADS = 16
N_KV = 4
HEAD_DIM = 64
Q_PER_KV = N_HEADS // N_KV
CMP_BLOCK = 64
SEL_BLOCK = CMP_BLOCK
N_SEL = 8
WINDOW = 512
SC_K = 3
LRU_K = 4
LRU_HEADS = 8
LRU_C = 8.0
SCALE = HEAD_DIM ** -0.5
LOG2E = 1.4426950408889634
EPS = 1e-6
NEG = -1e30
TINY = 1e-20
FORCE = 1e6

LANES = 128
SUBLANES = 8
VMEM_LIMIT = 56 * 1024 * 1024
GROUP_W = N_KV * HEAD_DIM
DEC_SAMPLES_PER_STEP = 4

F32 = jnp.float32
BF16 = jnp.bfloat16
_NT = (((1,), (1,)), ((), ()))


def _params(sem):
    return pltpu.CompilerParams(dimension_semantics=sem, vmem_limit_bytes=VMEM_LIMIT)


def _const_spec(shape):
    nd = len(shape)
    return pl.BlockSpec(shape, lambda *_: (0,) * nd, pipeline_mode=pl.Buffered(1))


def _rms(x, g):
    y = x * lax.rsqrt(jnp.mean(x * x, axis=-1, keepdims=True) + EPS)
    return y * g


def _dot(a, b):
    return jnp.dot(a, b, preferred_element_type=F32)


def _dot_nt(a, b):
    return lax.dot_general(a, b, _NT, preferred_element_type=F32)


def _qkv_prompt_kernel(*refs, tk, n_in):
    x_ref, g_ref, wqt_ref, wkvt_ref, wgt_ref, wkp_ref, wkc_ref, wc_ref = refs[:8]
    qt_ref, kvt_ref, kwt_ref, gt_ref, kp_ref, vt_ref, cmp_ref = refs[n_in:]
    tm = x_ref.shape[0]
    h = _rms(x_ref[...], g_ref[...]).astype(BF16)
    qt_ref[...] = (_dot_nt(wqt_ref[...], h) * (SCALE * LOG2E)).astype(BF16)
    kvt = _dot_nt(wkvt_ref[...], h)
    n_rows = kvt_ref.shape[0]
    kvt_ref[...] = kvt[:n_rows]
    kwt_ref[...] = kvt[n_rows:]
    gt_ref[...] = jax.nn.sigmoid(_dot_nt(wgt_ref[...], h))
    kp_ref[...] = _dot(h, wkp_ref[...]).astype(BF16)
    v_sel = kvt[3 * GROUP_W:4 * GROUP_W]
    v_win = kvt[5 * GROUP_W:]
    for c in range(tm // tk):
        cs = slice(c * tk, (c + 1) * tk)
        vt_ref[c, :GROUP_W] = v_sel[:, cs].astype(BF16)
        vt_ref[c, GROUP_W:] = v_win[:, cs].astype(BF16)
    width = wc_ref.shape[1]
    blocks = _dot(h, wkc_ref[...]).reshape(tm // CMP_BLOCK, CMP_BLOCK, width)
    cmp_ref[...] = jnp.sum(blocks * wc_ref[...][None], axis=1)


def _qkv_prompt(x, lw, batch, seq, tm, tk, layer, depth, kv_all):
    n, d = x.shape
    nt = seq // tm
    n_rows = 4 * GROUP_W
    n_win = 2 * GROUP_W
    nq = N_HEADS * HEAD_DIM
    row = lambda b, t: (b * nt + t, 0)
    colblk = lambda b, t: (b, 0, t)
    args = [x, lw["g1"], lw["wqt"], lw["wkvt"], lw["wgt"], lw["wkp"], lw["wkc"], lw["wc_p"]]
    in_specs = [pl.BlockSpec((tm, d), row)] + [_const_spec(a.shape) for a in args[1:]]
    aliases = {len(args): 1}
    args.append(kv_all)
    in_specs.append(pl.BlockSpec(memory_space=pl.ANY))
    n_gate = lw["wgt"].shape[0]
    n_kp = lw["wkp"].shape[1]
    n_cmp = lw["wc_p"].shape[1]
    return pl.pallas_call(
        functools.partial(_qkv_prompt_kernel, tk=tk, n_in=len(args)),
        grid=(batch, nt),
        in_specs=in_specs,
        input_output_aliases=aliases,
        out_specs=[pl.BlockSpec((None, nq, tm), colblk),
                   pl.BlockSpec((None, None, n_rows, tm), lambda b, t: (layer, b, 0, t)),
                   pl.BlockSpec((None, n_win, tm), colblk),
                   pl.BlockSpec((None, n_gate, tm), colblk),
                   pl.BlockSpec((tm, n_kp), row),
                   pl.BlockSpec((None, tm // tk, n_win, tk), lambda b, t: (b, t, 0, 0)),
                   pl.BlockSpec((tm // CMP_BLOCK, n_cmp), row)],
        out_shape=[jax.ShapeDtypeStruct((batch, nq, seq), BF16),
                   jax.ShapeDtypeStruct((depth, batch, n_rows, seq), F32),
                   jax.ShapeDtypeStruct((batch, n_win, seq), F32),
                   jax.ShapeDtypeStruct((batch, n_gate, seq), F32),
                   jax.ShapeDtypeStruct((n, n_kp), BF16),
                   jax.ShapeDtypeStruct((batch, seq // tk, n_win, tk), BF16),
                   jax.ShapeDtypeStruct((n // CMP_BLOCK, n_cmp), F32)],
        compiler_params=_params(("arbitrary", "arbitrary")), name="qkv_prompt")(*args)


def _qkv_decode_kernel(x_ref, g_ref, wq_ref, wkvt_ref, wg_ref, wkv_ref,
                       q_ref, kvt_ref, kwt_ref, gate_ref, kvr_ref):
    h = _rms(x_ref[...], g_ref[...]).astype(BF16)
    q_ref[...] = (_dot(h, wq_ref[...]) * SCALE).astype(BF16)
    kvt = _dot_nt(wkvt_ref[...], h)
    n_rows = kvt_ref.shape[0]
    kvt_ref[...] = kvt[:n_rows]
    kwt_ref[...] = kvt[n_rows:]
    gate_ref[...] = jax.nn.sigmoid(_dot(h, wg_ref[...]))
    kvr_ref[...] = _dot(h, wkv_ref[...])


def _qkv_decode(x, lw):
    n, d = x.shape
    args = [x, lw["g1"], lw["wq"], lw["wkvt"], lw["wg_s"], lw["wkv"]]
    full = lambda shape: pl.BlockSpec(shape, lambda i: (0,) * len(shape))
    shapes = [((n, lw["wq"].shape[1]), BF16), ((4 * GROUP_W, n), F32), ((2 * GROUP_W, n), F32),
              ((n, lw["wg_s"].shape[1]), F32), ((n, lw["wkv"].shape[1]), F32)]
    return pl.pallas_call(
        _qkv_decode_kernel,
        grid=(1,),
        in_specs=[full(x.shape)] + [_const_spec(a.shape) for a in args[1:]],
        out_specs=[full(s) for s, _ in shapes],
        out_shape=[jax.ShapeDtypeStruct(s, t) for s, t in shapes],
        compiler_params=_params(("arbitrary",)), name="qkv_decode")(*args)


def _cmp_topk_kernel(qt_ref, cmp_ref, cmpt_ref, gt_ref, oct_ref, selt_ref, qpad, *, n_sel):
    tq = qt_ref.shape[1]
    nblk = cmp_ref.shape[0]
    t0 = pl.program_id(2) * tq
    kcb = cmp_ref[...].astype(BF16)
    kct = cmpt_ref[...].astype(BF16)
    blk = lax.broadcasted_iota(jnp.int32, (nblk, tq), 0)
    tpos = t0 + lax.broadcasted_iota(jnp.int32, (nblk, tq), 1)
    valid = (blk + 1) * CMP_BLOCK - 1 <= tpos
    gt = gt_ref[...]
    qpad[HEAD_DIM:, :] = jnp.zeros((HEAD_DIM, tq), BF16)
    imp = jnp.zeros((nblk, tq), F32)
    for r in range(Q_PER_KV):
        hs = slice(r * HEAD_DIM, (r + 1) * HEAD_DIM)
        qpad[:HEAD_DIM, :] = qt_ref[hs, :]
        s = _dot(kcb, qpad[...])
        s = jnp.where(valid, s, NEG)
        m = jnp.max(s, axis=0, keepdims=True)
        p = jnp.where(valid, jnp.exp2(s - m), 0.0)
        p = p / jnp.maximum(jnp.sum(p, axis=0, keepdims=True), TINY)
        imp = imp + p
        o = _dot(kct, p.astype(BF16))
        oct_ref[hs, :] = gt[3 * r:3 * r + 1, :] * o[HEAD_DIM:, :]
    cur = lax.shift_right_logical(tpos, 6)
    imp = imp + jnp.where(blk == cur, FORCE, 0.0) + jnp.where(blk == 0, 0.5 * FORCE, 0.0)
    imp = jnp.where(blk <= cur, imp, -FORCE)
    blkf = blk.astype(F32)
    sel = jnp.zeros((nblk, tq), F32)
    for _ in range(n_sel):
        mx = jnp.max(imp, axis=0, keepdims=True)
        first = jnp.min(jnp.where(imp == mx, blkf, float(nblk)), axis=0, keepdims=True)
        hit = blkf == first
        sel = jnp.where(hit, 1.0, sel)
        imp = jnp.where(hit, -jnp.inf, imp)
    selt_ref[...] = jnp.where(blk <= cur, sel, 0.0).astype(BF16)


def _cmp_topk(qt, cmpk, cmpt, gates_t, batch, seq, tq):
    nblk = seq // CMP_BLOCK
    nq = seq // tq
    gw = Q_PER_KV * HEAD_DIM
    n_gate = gates_t.shape[1] // N_KV
    tile = lambda b, g, i: (b, g, i)
    return pl.pallas_call(
        functools.partial(_cmp_topk_kernel, n_sel=min(N_SEL, nblk)),
        grid=(batch, N_KV, nq),
        in_specs=[pl.BlockSpec((None, gw, tq), tile),
                  pl.BlockSpec((nblk, LANES), lambda b, g, i: (b, g)),
                  pl.BlockSpec((None, None, LANES, nblk), lambda b, g, i: (b, g, 0, 0)),
                  pl.BlockSpec((None, n_gate, tq), tile)],
        out_specs=[pl.BlockSpec((None, gw, tq), tile),
                   pl.BlockSpec((None, None, nblk, tq), lambda b, g, i: (g, b, 0, i))],
        out_shape=[jax.ShapeDtypeStruct((batch, N_HEADS * HEAD_DIM, seq), F32),
                   jax.ShapeDtypeStruct((N_KV, batch, nblk, seq), BF16)],
        scratch_shapes=[pltpu.VMEM((LANES, tq), BF16)],
        compiler_params=_params(("arbitrary", "arbitrary", "arbitrary")),
        name="cmp_topk")(qt, cmpk, cmpt, gates_t)


def _flash_kernel(qt_ref, kp_ref, vs_ref, vw_ref, selt_ref, gt_ref, oct_ref, o_ref,
                  qs_ref, qw_ref, bias_ref, s_ref, m_ref, l_ref, acc_ref, out_ref, *, n_back, qc):
    tq = qt_ref.shape[1]
    nblk = selt_ref.shape[0]
    tk = vs_ref.shape[2]
    i = pl.program_id(2)
    t0 = i * tq
    cols = Q_PER_KV * tq
    zeros = jnp.zeros((HEAD_DIM, cols), BF16)
    qs_ref[HEAD_DIM:, :] = zeros
    qw_ref[:HEAD_DIM, :] = zeros
    for r in range(Q_PER_KV):
        qh = qt_ref[r * HEAD_DIM:(r + 1) * HEAD_DIM, :]
        qs_ref[:HEAD_DIM, r * tq:(r + 1) * tq] = qh
        qw_ref[HEAD_DIM:, r * tq:(r + 1) * tq] = qh
    key0 = lax.broadcasted_iota(jnp.int32, (tk, tq), 0)
    tpos = t0 + lax.broadcasted_iota(jnp.int32, (tk, tq), 1)
    e_blk = lax.shift_right_logical(lax.broadcasted_iota(jnp.int32, (tk, nblk), 0), 6)
    e_col = lax.broadcasted_iota(jnp.int32, (tk, nblk), 1)

    def reset():
        m_ref[...] = jnp.full(m_ref.shape, NEG, F32)
        l_ref[...] = jnp.zeros(l_ref.shape, F32)
        acc_ref[...] = jnp.zeros(acc_ref.shape, F32)

    def key_tile(kt):
        return kp_ref[pl.ds(pl.multiple_of(kt * tk, tk), tk), :]

    def fill_scores(q_ref, kt):
        kp = key_tile(kt)
        for c in range(cols // qc):
            b0 = (c * qc) % tq
            s_ref[:, c * qc:(c + 1) * qc] = _dot(kp, q_ref[:, c * qc:(c + 1) * qc]) + bias_ref[:, b0:b0 + qc]

    def attend(q_ref, v_ref, kt, kt_next):
        kp = key_tile(kt_next)
        vt = v_ref[kt]
        pending = None
        for c in range(cols // qc):
            ls = slice(c * qc, (c + 1) * qc)
            b0 = (c * qc) % tq
            s = s_ref[:, ls]
            s_ref[:, ls] = _dot(kp, q_ref[:, ls]) + bias_ref[:, b0:b0 + qc]
            m_old = m_ref[:, ls]
            m_new = jnp.maximum(m_old, jnp.max(s, axis=0, keepdims=True))
            alpha = jnp.exp2(m_old - m_new)
            p = jnp.exp2(s - m_new)
            l_ref[:, ls] = alpha * l_ref[:, ls] + jnp.sum(p, axis=0, keepdims=True)
            m_ref[:, ls] = m_new
            pv = _dot(vt, p.astype(BF16))
            if pending is not None:
                pls, palpha, ppv = pending
                acc_ref[:, pls] = palpha * acc_ref[:, pls] + ppv
            pending = (ls, alpha, pv)
        pls, palpha, ppv = pending
        acc_ref[:, pls] = palpha * acc_ref[:, pls] + ppv

    def emit(k):
        gt = gt_ref[...]
        for r in range(Q_PER_KV):
            ls = slice(r * tq, (r + 1) * tq)
            hs = slice(r * HEAD_DIM, (r + 1) * HEAD_DIM)
            scale = gt[3 * r + k:3 * r + k + 1, :] / l_ref[:, ls]
            out_ref[hs, :] = out_ref[hs, :] + scale * acc_ref[:, ls]

    out_ref[...] = oct_ref[...]

    def sel_bias(kt):
        expand = jnp.where(e_col == kt * (tk // SEL_BLOCK) + e_blk, 1.0, 0.0).astype(BF16)
        chosen = _dot(expand, selt_ref[...])
        key = kt * tk + key0
        bias_ref[...] = jnp.where(chosen > 0.5, jnp.where(key <= tpos, 0.0, NEG), NEG)

    def sel_body(kt, carry):
        kt_next = jnp.minimum(kt + 1, i)
        sel_bias(kt_next)
        attend(qs_ref, vs_ref, kt, kt_next)
        return carry

    reset()
    sel_bias(0)
    fill_scores(qs_ref, 0)
    lax.fori_loop(0, i + 1, sel_body, 0)
    emit(1)

    def win_bias(kt):
        dlt = tpos - (kt * tk + key0)
        bias_ref[...] = jnp.where(dlt >= 0, jnp.where(dlt < WINDOW, 0.0, NEG), NEG)

    n_win = jnp.minimum(i, n_back) + 1

    def win_body(j, carry):
        kt = i - j
        kt_next = i - jnp.minimum(j + 1, n_win - 1)
        win_bias(kt_next)
        attend(qw_ref, vw_ref, kt, kt_next)
        return carry

    reset()
    win_bias(i)
    fill_scores(qw_ref, i)
    lax.fori_loop(0, n_win, win_body, 0)
    emit(2)
    o_ref[...] = out_ref[...].T.astype(BF16)


def _flash(qt, kp, vt, selt, gates_t, oct, batch, seq, tq, qc):
    nblk = seq // SEL_BLOCK
    nq = seq // tq
    nkt, tk = vt.shape[1], vt.shape[3]
    assert tk == tq
    gw = Q_PER_KV * HEAD_DIM
    n_gate = gates_t.shape[1] // N_KV
    n_back = -(-(WINDOW - 1) // tk)
    tile = lambda b, g, i: (b, g, i)
    v_spec = lambda kind: pl.BlockSpec((None, nkt, HEAD_DIM, tk),
                                       lambda b, g, i: (b, 0, kind * N_KV + g, 0))
    cols = Q_PER_KV * tq
    return pl.pallas_call(
        functools.partial(_flash_kernel, n_back=n_back, qc=qc),
        grid=(batch, N_KV, nq),
        in_specs=[pl.BlockSpec((None, gw, tq), tile),
                  pl.BlockSpec((seq, LANES), lambda b, g, i: (b, g)),
                  v_spec(0), v_spec(1),
                  pl.BlockSpec((None, None, nblk, tq), lambda b, g, i: (g, b, 0, i)),
                  pl.BlockSpec((None, n_gate, tq), tile),
                  pl.BlockSpec((None, gw, tq), tile)],
        out_specs=pl.BlockSpec((tq, gw), lambda b, g, i: (b * nq + i, g)),
        out_shape=jax.ShapeDtypeStruct((batch * seq, N_HEADS * HEAD_DIM), BF16),
        scratch_shapes=[pltpu.VMEM((LANES, cols), BF16), pltpu.VMEM((LANES, cols), BF16),
                        pltpu.VMEM((tk, tq), F32), pltpu.VMEM((tk, cols), F32),
                        pltpu.VMEM((1, cols), F32), pltpu.VMEM((1, cols), F32),
                        pltpu.VMEM((HEAD_DIM, cols), F32), pltpu.VMEM((gw, tq), F32)],
        compiler_params=_params(("arbitrary", "arbitrary", "arbitrary")),
        name="flash_sel_win")(qt, kp, vt, vt, selt, gates_t, oct)


def _shift_rows(x, d, fill):
    row = lax.broadcasted_iota(jnp.int32, x.shape, 0)
    return jnp.where(row < d, fill, pltpu.roll(x, d, axis=0))


def _lagged(u, tail, lag):
    us = pltpu.roll(u, lag, axis=0)
    row = lax.broadcasted_iota(jnp.int32, tail.shape, 0)
    top = jnp.where(row < lag, pltpu.roll(tail, lag, axis=0), us[:SUBLANES])
    return jnp.concatenate([top, us[SUBLANES:]], axis=0)


def _mix_tail(x, h, onsa, v_sc, v_lru, wgm_ref, wsco_ref, wlo_ref, wno_ref, wo_ref):
    d = x.shape[1]
    u_sc = _dot(v_sc.astype(BF16), wsco_ref[...])
    u_lru = _dot(v_lru.astype(BF16), wlo_ref[...])
    u_nsa = _dot(onsa, wno_ref[...])
    mix = jax.nn.sigmoid(_dot(h, wgm_ref[:, :d])) * u_nsa
    mix = mix + jax.nn.sigmoid(_dot(h, wgm_ref[:, d:2 * d])) * u_sc
    mix = mix + jax.nn.sigmoid(_dot(h, wgm_ref[:, 2 * d:])) * u_lru
    return x + _dot(mix.astype(BF16), wo_ref[...])


def _lru_coeffs(xc, wgate_ref, bgate_ref, lam_ref):
    w = xc.shape[1]
    gates = jax.nn.sigmoid(_dot(xc.astype(BF16), wgate_ref[...]) + bgate_ref[...])
    lam = -lam_ref[...]
    softplus = jnp.maximum(lam, 0.0) + jnp.log1p(jnp.exp(-jnp.abs(lam)))
    log_a = -LRU_C * gates[:, :w] * softplus
    a = jnp.exp(log_a)
    th = jnp.tanh(log_a)
    mult = jnp.sqrt(-2.0 * th / (1.0 - th))
    return a, mult, gates[:, w:]


def _mix_prompt_kernel(x_ref, onsa_ref, g_ref, wb_ref, wgm_ref, wscc_ref, wlc_ref, blc_ref,
                       wgate_ref, bgate_ref, lam_ref, wsco_ref, wlo_ref, wno_ref, wo_ref,
                       x1_ref, sct_ref, lct_ref, hl_ref, sc_tail, lc_tail, h_prev):
    t = pl.program_id(1)
    tm = x_ref.shape[0]
    w = sc_tail.shape[1]

    @pl.when(t == 0)
    def _():
        sc_tail[...] = jnp.zeros_like(sc_tail)
        lc_tail[...] = jnp.zeros_like(lc_tail)
        h_prev[...] = jnp.zeros_like(h_prev)

    x = x_ref[...]
    h = _rms(x, g_ref[...]).astype(BF16)
    proj = _dot(h, wb_ref[...])
    b_sc, c_sc, x_sc = proj[:, :w], proj[:, w:2 * w], proj[:, 2 * w:3 * w]
    x_lru, y_lru = proj[:, 3 * w:4 * w], proj[:, 4 * w:]

    cx = c_sc * x_sc
    wscc = wscc_ref[...]
    tail = sc_tail[...]
    conv = cx * wscc[SC_K - 1:SC_K]
    for lag in range(1, SC_K):
        conv = conv + _lagged(cx, tail, lag) * wscc[SC_K - 1 - lag:SC_K - lag]
    sc_tail[...] = cx[tm - SUBLANES:]
    sct_ref[...] = cx[tm - SUBLANES:]
    v_sc = b_sc * conv

    wlc = wlc_ref[...]
    tail = lc_tail[...]
    xc = x_lru * wlc[LRU_K - 1:LRU_K]
    for lag in range(1, LRU_K):
        xc = xc + _lagged(x_lru, tail, lag) * wlc[LRU_K - 1 - lag:LRU_K - lag]
    xc = xc + blc_ref[...]
    lc_tail[...] = x_lru[tm - SUBLANES:]
    lct_ref[...] = x_lru[tm - SUBLANES:]

    a, mult, gate_i = _lru_coeffs(xc, wgate_ref, bgate_ref, lam_ref)
    reset = (t * tm + lax.broadcasted_iota(jnp.int32, a.shape, 0)) == 0
    a = jnp.where(reset, 0.0, a)
    mult = jnp.where(reset, 1.0, mult)
    bx = mult * gate_i * xc
    d = 1
    while d < tm:
        bx = a * _shift_rows(bx, d, 0.0) + bx
        a = a * _shift_rows(a, d, 1.0)
        d *= 2
    hs = a * h_prev[...] + bx
    h_prev[...] = hs[tm - 1:]
    hl_ref[...] = hs[tm - SUBLANES:]
    v_lru = hs * jax.nn.gelu(y_lru)

    x1_ref[...] = _mix_tail(x, h, onsa_ref[...], v_sc, v_lru, wgm_ref, wsco_ref, wlo_ref, wno_ref, wo_ref)


def _mix_decode_kernel(x_ref, onsa_ref, scp_ref, lcp_ref, h0_ref, g_ref, wb_ref, wgm_ref, wscc_ref,
                       wlc_ref, blc_ref, wgate_ref, bgate_ref, lam_ref, wsco_ref, wlo_ref, wno_ref,
                       wo_ref, x1_ref, cx_ref, xl_ref, hn_ref):
    w = h0_ref.shape[1]
    x = x_ref[...]
    h = _rms(x, g_ref[...]).astype(BF16)
    proj = _dot(h, wb_ref[...])
    b_sc, c_sc, x_sc = proj[:, :w], proj[:, w:2 * w], proj[:, 2 * w:3 * w]
    x_lru, y_lru = proj[:, 3 * w:4 * w], proj[:, 4 * w:]

    cx = c_sc * x_sc
    wscc = wscc_ref[...]
    conv = cx * wscc[SC_K - 1:SC_K]
    for k in range(SC_K - 1):
        conv = conv + scp_ref[:, k * w:(k + 1) * w] * wscc[k:k + 1]
    cx_ref[...] = cx
    v_sc = b_sc * conv

    wlc = wlc_ref[...]
    xc = x_lru * wlc[LRU_K - 1:LRU_K]
    for k in range(LRU_K - 1):
        xc = xc + lcp_ref[:, k * w:(k + 1) * w] * wlc[k:k + 1]
    xc = xc + blc_ref[...]
    xl_ref[...] = x_lru

    a, mult, gate_i = _lru_coeffs(xc, wgate_ref, bgate_ref, lam_ref)
    hs = a * h0_ref[...] + mult * gate_i * xc
    hn_ref[...] = hs
    v_lru = hs * jax.nn.gelu(y_lru)

    x1_ref[...] = _mix_tail(x, h, onsa_ref[...], v_sc, v_lru, wgm_ref, wsco_ref, wlo_ref, wno_ref, wo_ref)


def _mix_weight_args(lw):
    return [lw["g1"], lw["wb"], lw["wgm"], lw["wscc"], lw["wlc"], lw["blc"], lw["wgate"], lw["bgate"],
            lw["lam"], lw["wsco"], lw["wlo"], lw["wno"], lw["wo"]]


def _mix_prompt(x, onsa, lw, batch, seq, tm):
    n, d = x.shape
    w = lw["wsco"].shape[0]
    nt = seq // tm
    row = lambda b, t: (b * nt + t, 0)
    per_b = lambda b, t: (b, 0, 0)
    wargs = _mix_weight_args(lw)
    tail_shape = jax.ShapeDtypeStruct((batch, SUBLANES, w), F32)
    tail_spec = pl.BlockSpec((None, SUBLANES, w), per_b)
    return pl.pallas_call(
        _mix_prompt_kernel,
        grid=(batch, nt),
        in_specs=[pl.BlockSpec((tm, d), row), pl.BlockSpec((tm, onsa.shape[1]), row)]
        + [_const_spec(a.shape) for a in wargs],
        out_specs=[pl.BlockSpec((tm, d), row), tail_spec, tail_spec, tail_spec],
        out_shape=[jax.ShapeDtypeStruct((n, d), F32), tail_shape, tail_shape, tail_shape],
        scratch_shapes=[pltpu.VMEM((SUBLANES, w), F32), pltpu.VMEM((SUBLANES, w), F32),
                        pltpu.VMEM((1, w), F32)],
        compiler_params=_params(("arbitrary", "arbitrary")),
        name="mix_prompt")(x, onsa, *wargs)


def _mix_decode(x, onsa, sc_prev, lc_prev, h0, lw):
    n, d = x.shape
    w = h0.shape[1]
    wargs = _mix_weight_args(lw)
    full = lambda a: pl.BlockSpec(a.shape, lambda i: (0,) * a.ndim)
    acts = [x, onsa, sc_prev, lc_prev, h0]
    state = jax.ShapeDtypeStruct((n, w), F32)
    return pl.pallas_call(
        _mix_decode_kernel,
        grid=(1,),
        in_specs=[full(a) for a in acts] + [_const_spec(a.shape) for a in wargs],
        out_specs=[pl.BlockSpec((n, d), lambda i: (0, 0))] + [pl.BlockSpec((n, w), lambda i: (0, 0))] * 3,
        out_shape=[jax.ShapeDtypeStruct((n, d), F32), state, state, state],
        compiler_params=_params(("arbitrary",)),
        name="mix_decode")(*acts, *wargs)


def _ffn_kernel(*refs, final, n_chunks):
    if final:
        x_ref, g_ref, wi_ref, wout_ref, gf_ref, y_ref = refs
    else:
        x_ref, g_ref, wi_ref, wout_ref, y_ref = refs
    x = x_ref[...]
    h = _rms(x, g_ref[...]).astype(BF16)
    d_ff = wout_ref.shape[0]
    ck = d_ff // n_chunks
    y = x
    for c in range(n_chunks):
        gate = _dot(h, wi_ref[:, c * ck:(c + 1) * ck])
        up = _dot(h, wi_ref[:, d_ff + c * ck:d_ff + (c + 1) * ck])
        y = y + _dot((jax.nn.silu(gate) * up).astype(BF16), wout_ref[c * ck:(c + 1) * ck, :])
    if final:
        y = _rms(y, gf_ref[...])
    y_ref[...] = y


def _ffn(x, g, wi, wout, gf, tm):
    n, d = x.shape
    final = gf is not None
    d_ff = wout.shape[0]
    n_chunks = 2 if d_ff % (2 * LANES) == 0 else 1
    row = lambda i: (i, 0)
    args = [x, g, wi, wout] + ([gf] if final else [])
    return pl.pallas_call(
        functools.partial(_ffn_kernel, final=final, n_chunks=n_chunks),
        grid=(n // tm,),
        in_specs=[pl.BlockSpec((tm, d), row)] + [_const_spec(a.shape) for a in args[1:]],
        out_specs=pl.BlockSpec((tm, d), row),
        out_shape=jax.ShapeDtypeStruct((n, d), F32),
        compiler_params=_params(("arbitrary",)),
        name="ffn")(*args)


def _group_onehot(rows, cols, row_div, col_div):
    r = lax.broadcasted_iota(jnp.int32, (rows, cols), 0) // row_div
    c = lax.broadcasted_iota(jnp.int32, (rows, cols), 1) // col_div
    return r == c


def _tile_lanes(q, reps):
    hd = q.shape[1]
    r = lax.broadcasted_iota(jnp.int32, (hd, reps * hd), 0)
    c = lax.broadcasted_iota(jnp.int32, (hd, reps * hd), 1)
    rep = jnp.where(c % hd == r, 1.0, 0.0).astype(BF16)
    return _dot(q, rep)


def _fold_heads(o):
    own = _group_onehot(o.shape[0], o.shape[1], Q_PER_KV, HEAD_DIM)
    o = jnp.where(own, o, 0.0)
    out = o[:, :HEAD_DIM]
    for g in range(1, o.shape[1] // HEAD_DIM):
        out = out + o[:, g * HEAD_DIM:(g + 1) * HEAD_DIM]
    return out


def _dec_cmp_kernel(pt_ref, q_ref, gate_ref, wct_ref, seg_ref, cache_ref, oc_ref, idx_ref, buf, sem,
                    *, row0, n_pages, page, past, n_sel):
    step = pl.program_id(0)
    spb = q_ref.shape[0]

    def copies(st, slot):
        out = []
        for u in range(spb):
            for pg in range(n_pages):
                base = (row0 + pt_ref[(st * spb + u) * n_pages + pg]) * 4
                for kind in range(2):
                    out.append(pltpu.make_async_copy(cache_ref.at[base + kind],
                                                     buf.at[slot, u, kind, :, :, pl.ds(pg * page, page)],
                                                     sem.at[slot]))
        return out

    slot = step % 2

    @pl.when(step == 0)
    def _():
        for c in copies(0, 0):
            c.start()

    @pl.when(step + 1 < pl.num_programs(0))
    def _():
        for c in copies(step + 1, 1 - slot):
            c.start()

    for c in copies(step, slot):
        c.wait()

    for u in range(spb):
        _dec_cmp_sample(q_ref.at[u], gate_ref.at[u], wct_ref, seg_ref, oc_ref.at[u], idx_ref.at[u],
                        buf.at[slot, u], n_pages=n_pages, page=page, past=past, n_sel=n_sel)


def _dec_cmp_sample(q_ref, gate_ref, wct_ref, seg_ref, oc_ref, idx_ref, buf, *, n_pages, page, past, n_sel):
    npad = seg_ref.shape[1]
    kchunk = 4 * page
    cmp_t = []
    for kind in range(2):
        w = jnp.concatenate([wct_ref[kind]] * (kchunk // page), axis=1)
        acc = jnp.zeros((GROUP_W, npad), F32)
        for c in range(past // kchunk):
            cs = slice(c * kchunk, (c + 1) * kchunk)
            y = buf[kind, :, :, cs].reshape(GROUP_W, kchunk) * w
            hi = y.astype(BF16)
            lo = (y - hi.astype(F32)).astype(BF16)
            seg = seg_ref[cs, :]
            acc = acc + _dot(hi, seg) + _dot(lo, seg)
        cmp_t.append(acc.astype(BF16))
    kc_t, vc_t = cmp_t

    q = q_ref[...]
    own = _group_onehot(N_HEADS, GROUP_W, Q_PER_KV, HEAD_DIM)
    qbd = jnp.where(own, _tile_lanes(q, N_KV), 0.0).astype(BF16)
    s = _dot(qbd, kc_t)
    blk = lax.broadcasted_iota(jnp.int32, (N_HEADS, npad), 1)
    valid = (blk + 1) * CMP_BLOCK - 1 <= past
    s = jnp.where(valid, s, NEG)
    m = jnp.max(s, axis=-1, keepdims=True)
    p = jnp.where(valid, jnp.exp(s - m), 0.0)
    p = p / jnp.maximum(jnp.sum(p, axis=-1, keepdims=True), TINY)
    o = _fold_heads(_dot_nt(p.astype(BF16), vc_t))
    oc_ref[...] = gate_ref[:, 0:1] * o

    blk8 = lax.broadcasted_iota(jnp.int32, (SUBLANES, npad), 1)
    row8 = lax.broadcasted_iota(jnp.int32, (SUBLANES, npad), 0)
    imp = jnp.zeros((SUBLANES, npad), F32)
    for g in range(N_KV):
        psum = jnp.sum(p[g * Q_PER_KV:(g + 1) * Q_PER_KV], axis=0, keepdims=True)
        imp = jnp.where(row8 == g, psum, imp)
    cur = past // SEL_BLOCK
    imp = imp + jnp.where(blk8 == cur, FORCE, 0.0) + jnp.where(blk8 == 0, 0.5 * FORCE, 0.0)
    imp = jnp.where(blk8 <= cur, imp, -jnp.inf)
    blkf = blk8.astype(F32)
    idx = jnp.zeros((SUBLANES, LANES), F32)
    lane = lax.broadcasted_iota(jnp.int32, (SUBLANES, LANES), 1)
    for j in range(n_sel):
        mx = jnp.max(imp, axis=-1, keepdims=True)
        first = jnp.min(jnp.where(imp == mx, blkf, float(npad)), axis=-1, keepdims=True)
        idx = jnp.where(lane == j, first, idx)
        imp = jnp.where(blkf == first, -jnp.inf, imp)
    idx_ref[...] = idx.astype(jnp.int32)


def _dec_cmp(page_table, q3, gates3, wct, cache4, row0, past):
    db = q3.shape[0]
    n_pages = page_table.shape[1]
    page = cache4.shape[3]
    n_sel = min(N_SEL, past // SEL_BLOCK + 1)
    npad = -(-(past // CMP_BLOCK + 1) // LANES) * LANES
    seg = (jnp.arange(past)[:, None] // CMP_BLOCK == jnp.arange(npad)[None, :]).astype(BF16)
    spb = DEC_SAMPLES_PER_STEP if db % DEC_SAMPLES_PER_STEP == 0 else 1
    per_b = lambda b, pt: (b, 0, 0)
    grid_spec = pltpu.PrefetchScalarGridSpec(
        num_scalar_prefetch=1,
        grid=(db // spb,),
        in_specs=[pl.BlockSpec((spb, N_HEADS, HEAD_DIM), per_b),
                  pl.BlockSpec((spb, N_HEADS, LANES), per_b),
                  pl.BlockSpec(wct.shape, lambda b, pt: (0, 0, 0)),
                  pl.BlockSpec(seg.shape, lambda b, pt: (0, 0)),
                  pl.BlockSpec(memory_space=pl.ANY)],
        out_specs=[pl.BlockSpec((spb, N_HEADS, HEAD_DIM), per_b),
                   pl.BlockSpec((spb, SUBLANES, LANES), per_b)],
        scratch_shapes=[pltpu.VMEM((2, spb, 2, N_KV, HEAD_DIM, n_pages * page), F32),
                        pltpu.SemaphoreType.DMA((2,))])
    return pl.pallas_call(
        functools.partial(_dec_cmp_kernel, row0=row0, n_pages=n_pages, page=page, past=past, n_sel=n_sel),
        grid_spec=grid_spec,
        out_shape=[jax.ShapeDtypeStruct((db, N_HEADS, HEAD_DIM), F32),
                   jax.ShapeDtypeStruct((db, SUBLANES, LANES), jnp.int32)],
        compiler_params=_params(("arbitrary",)),
        name="dec_cmp_topk")(page_table.reshape(-1), q3, gates3, wct, seg, cache4)


def _dec_attn_kernel(idx_ref, pt_ref, q_ref, gate_ref, oc_ref, kvn_ref, kwcol_ref, win_ref, cache_ref,
                     *rest, row0, n_sel, past, page):
    o_ref, wout_ref, kvbuf, sem = rest[-4:]
    step = pl.program_id(0)
    spb = q_ref.shape[0]
    nblk_past = past // SEL_BLOCK
    per_page = page // SEL_BLOCK
    n_pages = nblk_past // per_page

    def block_copies(bb, slot, u, g, j, n):
        base = (row0 + pt_ref[bb * n_pages + n // per_page]) * 4
        dst = pl.ds(j * page, page)
        return pltpu.make_async_copy(cache_ref.at[pl.ds(base + 2, 2), g], kvbuf.at[slot, u, g, :, :, dst],
                                     sem.at[slot])

    def gather(st, slot, wait):
        for u in range(spb):
            bb = st * spb + u
            for g in range(N_KV):
                for j in range(n_sel):
                    n = jnp.minimum(idx_ref[(bb * N_KV + g) * n_sel + j], nblk_past - 1)
                    c = block_copies(bb, slot, u, g, j, n)
                    c.wait() if wait else c.start()

    slot = step % 2

    @pl.when(step == 0)
    def _():
        gather(0, 0, wait=False)

    @pl.when(step + 1 < pl.num_programs(0))
    def _():
        gather(step + 1, 1 - slot, wait=False)

    gather(step, slot, wait=True)
    for u in range(spb):
        _dec_attn_sample(step * spb + u, idx_ref, q_ref.at[u], gate_ref.at[u], oc_ref.at[u], kvn_ref.at[u],
                         kwcol_ref, win_ref.at[u], o_ref.at[u], wout_ref.at[u], kvbuf.at[slot, u],
                         n_sel=n_sel, past=past, page=page)


def _dec_attn_sample(b, idx_ref, q_ref, gate_ref, oc_ref, kvn_ref, kwcol_ref, win_ref, o_ref, wout_ref,
                     kvbuf, *, n_sel, past, page):
    nblk_past = past // SEL_BLOCK
    per_page = page // SEL_BLOCK
    win_len = win_ref.shape[1]
    q = q_ref[...]
    own = _group_onehot(N_HEADS, GROUP_W, Q_PER_KV, HEAD_DIM)
    qbd_f = jnp.where(own, _tile_lanes(q, N_KV), 0.0)
    qbd = qbd_f.astype(BF16)
    gt = gate_ref[...]
    kvn = kvn_ref[...]

    win = win_ref[...]
    s = _dot(qbd, win[:GROUP_W].astype(BF16))
    colw = lax.broadcasted_iota(jnp.int32, s.shape, 1)
    s = jnp.where(win_len - colw < WINDOW, s, NEG)
    s_new = jnp.sum(qbd_f * kvn[:, 4 * GROUP_W:5 * GROUP_W], axis=-1, keepdims=True)
    m = jnp.maximum(jnp.max(s, axis=-1, keepdims=True), s_new)
    p = jnp.exp(s - m)
    p_new = jnp.exp(s_new - m)
    l = jnp.sum(p, axis=-1, keepdims=True) + p_new
    o_w = _fold_heads(_dot_nt(p.astype(BF16), win[GROUP_W:].astype(BF16))
                      + p_new * kvn[:, 5 * GROUP_W:]) / l

    lane_b = lax.broadcasted_iota(jnp.int32, kwcol_ref.shape, 1)
    new_col = jnp.sum(jnp.where(lane_b == b, kwcol_ref[...], 0.0), axis=1, keepdims=True)
    pos = lax.broadcasted_iota(jnp.int32, win.shape, 1)
    wout_ref[...] = jnp.where(pos == win_len - 1, new_col, pltpu.roll(win, win_len - 1, axis=1))

    nkeys = n_sel * page
    rowg = lax.broadcasted_iota(jnp.int32, (N_HEADS, HEAD_DIM), 0) // Q_PER_KV
    rows = lax.broadcasted_iota(jnp.int32, (N_HEADS, nkeys), 0) // Q_PER_KV
    lanes = lax.broadcasted_iota(jnp.int32, (N_HEADS, nkeys), 1)
    colj = lanes // page
    in_page = (lanes % page) // SEL_BLOCK
    qf = q.astype(F32)
    s = jnp.zeros((N_HEADS, nkeys), F32)
    okay = jnp.zeros((N_HEADS, nkeys), F32)
    for g in range(N_KV):
        qg = jnp.where(rowg == g, qf, 0.0).astype(BF16)
        s = s + _dot(qg, kvbuf[g, 0].astype(BF16))
        for j in range(n_sel):
            n = idx_ref[(b * N_KV + g) * n_sel + j]
            flag = jnp.where(n < nblk_past, 1.0, 0.0)
            here = jnp.where(in_page == n % per_page, flag, 0.0)
            okay = jnp.where(rows == g, jnp.where(colj == j, here, okay), okay)
    s = jnp.where(okay > 0.5, s, NEG)
    s_new = jnp.sum(qbd_f * kvn[:, 2 * GROUP_W:3 * GROUP_W], axis=-1, keepdims=True)
    m = jnp.maximum(jnp.max(s, axis=-1, keepdims=True), s_new)
    p = jnp.where(okay > 0.5, jnp.exp(s - m), 0.0)
    p_new = jnp.exp(s_new - m)
    l = jnp.sum(p, axis=-1, keepdims=True) + p_new
    o_sel = jnp.zeros((N_HEADS, HEAD_DIM), F32)
    for g in range(N_KV):
        pg = jnp.where(rows == g, p, 0.0).astype(BF16)
        o_sel = o_sel + _dot_nt(pg, kvbuf[g, 1].astype(BF16))
    o_s = (o_sel + _fold_heads(p_new * jnp.where(own, kvn[:, 3 * GROUP_W:4 * GROUP_W], 0.0))) / l

    o_ref[...] = (oc_ref[...] + gt[:, 1:2] * o_s + gt[:, 2:3] * o_w).astype(BF16)


def _dec_attn(idx, page_table, q3, gates3, oc, kv_new, kw_cols, win, cache4, row0, win_row0, past, win_all):
    db = q3.shape[0]
    page = cache4.shape[3]
    n_sel = idx.shape[-1]
    spb = DEC_SAMPLES_PER_STEP if db % DEC_SAMPLES_PER_STEP == 0 else 1
    operands = [idx.reshape(-1), page_table.reshape(-1), q3, gates3, oc, kv_new[:, None, :], kw_cols, win,
                cache4, win_all]
    per_b = lambda b, *_: (b, 0, 0)
    win_b = lambda b, *_: (win_row0 // spb + b, 0, 0)
    win_block = (spb,) + win.shape[1:]
    grid_spec = pltpu.PrefetchScalarGridSpec(
        num_scalar_prefetch=2,
        grid=(db // spb,),
        in_specs=[pl.BlockSpec((spb, N_HEADS, HEAD_DIM), per_b),
                  pl.BlockSpec((spb, N_HEADS, LANES), per_b),
                  pl.BlockSpec((spb, N_HEADS, HEAD_DIM), per_b),
                  pl.BlockSpec((spb, 1, kv_new.shape[-1]), per_b),
                  pl.BlockSpec(kw_cols.shape, lambda b, *_: (0, 0)),
                  pl.BlockSpec(win_block, win_b),
                  pl.BlockSpec(memory_space=pl.ANY), pl.BlockSpec(memory_space=pl.ANY)],
        out_specs=[pl.BlockSpec((spb, N_HEADS, HEAD_DIM), per_b),
                   pl.BlockSpec(win_block, win_b)],
        scratch_shapes=[pltpu.VMEM((2, spb, N_KV, 2, HEAD_DIM, n_sel * page), F32),
                        pltpu.SemaphoreType.DMA((2,))])
    return pl.pallas_call(
        functools.partial(_dec_attn_kernel, row0=row0, n_sel=n_sel, past=past, page=page),
        grid_spec=grid_spec,
        out_shape=[jax.ShapeDtypeStruct((db, N_HEADS, HEAD_DIM), BF16),
                   jax.ShapeDtypeStruct(win.shape, F32)],
        input_output_aliases={len(operands) - 1: 1},
        compiler_params=_params(("arbitrary",)),
        name="dec_attn")(*operands)


def _layer_weights(l, norm1_g, w_in, w_cmp_k, w_cmp_v, w_nsa_out, w_sc_conv, w_sc_out, w_lru_conv,
                   b_lru_conv, w_lru_gate, b_lru_gate, lru_lambda, w_lru_out, w_o, norm2_g, w_ffn_in,
                   w_ffn_out):
    d = w_in.shape[1]
    nsa = N_HEADS * HEAD_DIM
    kvc = 2 * GROUP_W
    sc_w = w_sc_out.shape[1]
    lru_w = w_lru_out.shape[1]
    wi = w_in[l]
    o = 0
    wq = wi[:, o:o + nsa]; o += nsa
    wkv = wi[:, o:o + 3 * kvc]; o += 3 * kvc
    wg = wi[:, o:o + 3 * N_HEADS]; o += 3 * N_HEADS
    wb = wi[:, o:o + 3 * sc_w + 2 * lru_w]; o += 3 * sc_w + 2 * lru_w
    wgm = wi[:, o:]
    wkc = wkv[:, :kvc].reshape(d, 2, N_KV, HEAD_DIM).transpose(0, 2, 1, 3).reshape(d, kvc)
    k_slc = wkv[:, kvc:kvc + GROUP_W].reshape(d, N_KV, 1, HEAD_DIM)
    k_win = wkv[:, 2 * kvc:2 * kvc + GROUP_W].reshape(d, N_KV, 1, HEAD_DIM)
    wkp = jnp.concatenate([k_slc, k_win], axis=2).reshape(d, 2 * GROUP_W)
    pad_g = ((0, 0), (0, 0), (0, 2 * SUBLANES - 3 * Q_PER_KV))
    wgt = jnp.pad(wg.reshape(d, N_KV, 3 * Q_PER_KV), pad_g).reshape(d, N_KV * 2 * SUBLANES).T
    pad_h = ((0, 0), (0, 0), (0, LANES - 3))
    wg_s = jnp.pad(wg.reshape(d, N_HEADS, 3), pad_h).reshape(d, N_HEADS * LANES)
    ck, cv = w_cmp_k[l], w_cmp_v[l]
    wc_p = jnp.tile(jnp.concatenate([ck, cv], axis=1), (1, N_KV))
    wct = jnp.stack([jnp.tile(ck.T, (N_KV, 2)), jnp.tile(cv.T, (N_KV, 2))])
    eye = jnp.eye(LRU_HEADS, dtype=F32)
    wgate = jnp.concatenate(
        [(eye[:, None, :, None] * w_lru_gate[l, k][:, :, None, :]).reshape(lru_w, lru_w) for k in range(2)],
        axis=1)
    bf = lambda a: a.astype(BF16)
    return {
        "g1": norm1_g[l][None], "wq": bf(wq), "wqt": bf(wq.T), "wkvt": bf(wkv.T), "wkv": bf(wkv),
        "wkc": bf(wkc), "wkp": bf(wkp), "wgt": bf(wgt), "wg_s": bf(wg_s), "wc_p": wc_p, "wct": wct,
        "wb": bf(wb), "wgm": bf(wgm), "wscc": w_sc_conv[l], "wlc": w_lru_conv[l],
        "blc": b_lru_conv[l][None], "wgate": bf(wgate), "bgate": b_lru_gate[l].reshape(1, 2 * lru_w),
        "lam": lru_lambda[l][None], "wsco": bf(w_sc_out[l]), "wlo": bf(w_lru_out[l]),
        "wno": bf(w_nsa_out[l]), "wo": bf(w_o[l]), "g2": norm2_g[l][None],
        "wfi": bf(w_ffn_in[l]), "wfo": bf(w_ffn_out[l]),
    }


def _pick_tile(n, pref):
    t = min(n, pref)
    while n % t:
        t //= 2
    return t


def kernel(x_prompt, x_sample, cache_kv, cache_win, state_sconv, state_lru_conv, state_lru_h, page_table, norm1_g, w_in, w_cmp_k, w_cmp_v, w_nsa_out, w_sc_conv, w_sc_out, w_lru_conv, b_lru_conv, w_lru_gate, b_lru_gate, lru_lambda, w_lru_out, w_o, norm2_g, w_ffn_in, w_ffn_out, final_g):
    batch, seq, d = x_prompt.shape
    db, dec_seq, _ = x_sample.shape
    depth = w_in.shape[0]
    n_pool, page = cache_kv.shape[1], cache_kv.shape[2]
    past = page_table.shape[1] * page
    win_len = cache_win.shape[2]
    assert dec_seq == 1 and past % SEL_BLOCK == 0 and page % SEL_BLOCK == 0
    assert seq % 256 == 0 and page == LANES

    tq = 256
    tm_qkv = _pick_tile(seq, 512)
    tm_mix = _pick_tile(seq, 256)
    tm_ffn = _pick_tile(batch * seq, 512)
    gf = final_g[None]

    cache4 = jnp.transpose(cache_kv, (0, 1, 3, 4, 5, 2)).reshape(depth * n_pool * 4, N_KV, HEAD_DIM, page)
    win_t = jnp.transpose(cache_win, (0, 1, 3, 4, 5, 2)).reshape(depth * db, 2 * GROUP_W, win_len)

    xp = x_prompt.reshape(batch * seq, d)
    xs = x_sample.reshape(db, d)
    outs_p, outs_s = [], []
    kv_all = jnp.zeros((depth, batch, 4 * GROUP_W, seq), F32)
    win_all = jnp.zeros(win_t.shape, F32)
    for l in range(depth):
        lw = _layer_weights(l, norm1_g, w_in, w_cmp_k, w_cmp_v, w_nsa_out, w_sc_conv, w_sc_out,
                            w_lru_conv, b_lru_conv, w_lru_gate, b_lru_gate, lru_lambda, w_lru_out,
                            w_o, norm2_g, w_ffn_in, w_ffn_out)
        last = l == depth - 1
        qt, kv_all, kwt, gates_t, kp, vt, cmpk = _qkv_prompt(xp, lw, batch, seq, tm_qkv, tq, l, depth, kv_all)
        cmpt = cmpk.reshape(batch, seq // CMP_BLOCK, N_KV, LANES).transpose(0, 2, 3, 1)
        oct, selt = _cmp_topk(qt, cmpk, cmpt, gates_t, batch, seq, tq)
        onsa = _flash(qt, kp, vt, selt, gates_t, oct, batch, seq, tq, qc=LANES)
        x1, sct, lct, hl = _mix_prompt(xp, onsa, lw, batch, seq, tm_mix)
        xp = _ffn(x1, lw["g2"], lw["wfi"], lw["wfo"], gf if last else None, tm_ffn)
        w_keep = min(WINDOW, seq)
        outs_p.append((None, kwt[:, :, seq - w_keep:], sct[:, SUBLANES - (SC_K - 1):],
                       lct[:, SUBLANES - (LRU_K - 1):], hl[:, SUBLANES - 1]))
        q, kvt, kwt, gates, kvr = _qkv_decode(xs, lw)
        q3 = q.reshape(db, N_HEADS, HEAD_DIM)
        gates3 = gates.reshape(db, N_HEADS, LANES)
        oc, idx = _dec_cmp(page_table, q3, gates3, lw["wct"], cache4, l * n_pool, past)
        n_sel = min(N_SEL, past // SEL_BLOCK + 1)
        idx = idx[:, :N_KV, :n_sel]
        onsa, win_all = _dec_attn(idx, page_table, q3, gates3, oc, kvr, kwt, win_t, cache4,
                                  l * n_pool, l * db, past, win_all)
        sc_prev = state_sconv[l].reshape(db, -1)
        lc_prev = state_lru_conv[l].reshape(db, -1)
        x1, cx, xl, hn = _mix_decode(xs, onsa.reshape(db, N_HEADS * HEAD_DIM), sc_prev, lc_prev,
                                     state_lru_h[l], lw)
        xs = _ffn(x1, lw["g2"], lw["wfi"], lw["wfo"], gf if last else None, db)
        outs_s.append((kvt, None,
                       jnp.concatenate([state_sconv[l][:, 1:], cx[:, None]], axis=1),
                       jnp.concatenate([state_lru_conv[l][:, 1:], xl[:, None]], axis=1),
                       hn))
    stack = lambda outs, i: jnp.stack([o[i] for o in outs])
    rows_out = lambda a, kinds: jnp.transpose(
        a.reshape(a.shape[:2] + (kinds, N_KV, HEAD_DIM, a.shape[-1])), (0, 1, 5, 2, 3, 4))
    y_prompt = xp.reshape(batch, seq, d)
    y_sample = xs.reshape(db, 1, d)
    kv_s = jnp.transpose(stack(outs_s, 0).reshape(depth, 1, 4, N_KV, HEAD_DIM, db), (0, 5, 1, 2, 3, 4))
    return (y_prompt, y_sample,
            rows_out(kv_all, 4), rows_out(stack(outs_p, 1), 2),
            stack(outs_p, 2), stack(outs_p, 3), stack(outs_p, 4),
            kv_s, rows_out(win_all.reshape((depth, db) + win_all.shape[1:]), 2),
            stack(outs_s, 2), stack(outs_s, 3), stack(outs_s, 4))
```

```python
import functools

import jax
import jax.numpy as jnp
from jax import lax
from jax.experimental import pallas as pl
from jax.experimental.pallas import tpu as pltpu

N_HEADS = 16
N_KV = 4
HEAD_DIM = 64
Q_PER_KV = N_HEADS // N_KV
CMP_BLOCK = 64
SEL_BLOCK = CMP_BLOCK
N_SEL = 8
WINDOW = 512
SC_K = 3
LRU_K = 4
LRU_HEADS = 8
LRU_C = 8.0
SCALE = HEAD_DIM ** -0.5
LOG2E = 1.4426950408889634
EPS = 1e-6
NEG = -1e30
TINY = 1e-20
FORCE = 1e6
MASK_BIG = 2.0 ** 100

LANES = 128
SUBLANES = 8
VMEM_LIMIT = 56 * 1024 * 1024
GROUP_W = N_KV * HEAD_DIM
DEC_SAMPLES_PER_STEP = 4

F32 = jnp.float32
BF16 = jnp.bfloat16
_NT = (((1,), (1,)), ((), ()))


def _params(sem):
    return pltpu.CompilerParams(dimension_semantics=sem, vmem_limit_bytes=VMEM_LIMIT)


def _const_spec(shape):
    nd = len(shape)
    return pl.BlockSpec(shape, lambda *_: (0,) * nd, pipeline_mode=pl.Buffered(1))


def _rms(x, g):
    y = x * lax.rsqrt(jnp.mean(x * x, axis=-1, keepdims=True) + EPS)
    return y * g


def _dot(a, b):
    return jnp.dot(a, b, preferred_element_type=F32)


def _dot_nt(a, b):
    return lax.dot_general(a, b, _NT, preferred_element_type=F32)


def _qkv_prompt_kernel(*refs, tk, n_in):
    x_ref, g_ref, wqt_ref, wkvt_ref, wgt_ref, wkp_ref, wkc_ref, wc_ref = refs[:8]
    qt_ref, kvt_ref, kwt_ref, gt_ref, kp_ref, vt_ref, cmp_ref = refs[n_in:]
    tm = x_ref.shape[0]
    h = _rms(x_ref[...], g_ref[...]).astype(BF16)
    qt_ref[...] = (_dot_nt(wqt_ref[...], h) * (SCALE * LOG2E)).astype(BF16)
    kvt = _dot_nt(wkvt_ref[...], h)
    n_rows = kvt_ref.shape[0]
    kvt_ref[...] = kvt[:n_rows]
    kwt_ref[...] = kvt[n_rows:]
    gt_ref[...] = jax.nn.sigmoid(_dot_nt(wgt_ref[...], h))
    kp_ref[...] = _dot(h, wkp_ref[...]).astype(BF16)
    v_sel = kvt[3 * GROUP_W:4 * GROUP_W]
    v_win = kvt[5 * GROUP_W:]
    for c in range(tm // tk):
        cs = slice(c * tk, (c + 1) * tk)
        vt_ref[c, :GROUP_W] = v_sel[:, cs].astype(BF16)
        vt_ref[c, GROUP_W:] = v_win[:, cs].astype(BF16)
    width = wc_ref.shape[1]
    blocks = _dot(h, wkc_ref[...]).reshape(tm // CMP_BLOCK, CMP_BLOCK, width)
    cmp_ref[...] = jnp.sum(blocks * wc_ref[...][None], axis=1)


def _qkv_prompt(x, lw, batch, seq, tm, tk, layer, depth, kv_all):
    n, d = x.shape
    nt = seq // tm
    n_rows = 4 * GROUP_W
    n_win = 2 * GROUP_W
    nq = N_HEADS * HEAD_DIM
    row = lambda b, t: (b * nt + t, 0)
    colblk = lambda b, t: (b, 0, t)
    args = [x, lw["g1"], lw["wqt"], lw["wkvt"], lw["wgt"], lw["wkp"], lw["wkc"], lw["wc_p"]]
    in_specs = [pl.BlockSpec((tm, d), row)] + [_const_spec(a.shape) for a in args[1:]]
    aliases = {len(args): 1}
    args.append(kv_all)
    in_specs.append(pl.BlockSpec(memory_space=pl.ANY))
    n_gate = lw["wgt"].shape[0]
    n_kp = lw["wkp"].shape[1]
    n_cmp = lw["wc_p"].shape[1]
    return pl.pallas_call(
        functools.partial(_qkv_prompt_kernel, tk=tk, n_in=len(args)),
        grid=(batch, nt),
        in_specs=in_specs,
        input_output_aliases=aliases,
        out_specs=[pl.BlockSpec((None, nq, tm), colblk),
                   pl.BlockSpec((None, None, n_rows, tm), lambda b, t: (layer, b, 0, t)),
                   pl.BlockSpec((None, n_win, tm), colblk),
                   pl.BlockSpec((None, n_gate, tm), colblk),
                   pl.BlockSpec((tm, n_kp), row),
                   pl.BlockSpec((None, tm // tk, n_win, tk), lambda b, t: (b, t, 0, 0)),
                   pl.BlockSpec((tm // CMP_BLOCK, n_cmp), row)],
        out_shape=[jax.ShapeDtypeStruct((batch, nq, seq), BF16),
                   jax.ShapeDtypeStruct((depth, batch, n_rows, seq), F32),
                   jax.ShapeDtypeStruct((batch, n_win, seq), F32),
                   jax.ShapeDtypeStruct((batch, n_gate, seq), F32),
                   jax.ShapeDtypeStruct((n, n_kp), BF16),
                   jax.ShapeDtypeStruct((batch, seq // tk, n_win, tk), BF16),
                   jax.ShapeDtypeStruct((n // CMP_BLOCK, n_cmp), F32)],
        compiler_params=_params(("arbitrary", "arbitrary")), name="qkv_prompt")(*args)


def _qkv_decode_kernel(x_ref, g_ref, wq_ref, wkvt_ref, wg_ref, wkv_ref,
                       q_ref, kvt_ref, kwt_ref, gate_ref, kvr_ref):
    h = _rms(x_ref[...], g_ref[...]).astype(BF16)
    q_ref[...] = (_dot(h, wq_ref[...]) * SCALE).astype(BF16)
    kvt = _dot_nt(wkvt_ref[...], h)
    n_rows = kvt_ref.shape[0]
    kvt_ref[...] = kvt[:n_rows]
    kwt_ref[...] = kvt[n_rows:]
    gate_ref[...] = jax.nn.sigmoid(_dot(h, wg_ref[...]))
    kvr_ref[...] = _dot(h, wkv_ref[...])


def _qkv_decode(x, lw):
    n, d = x.shape
    args = [x, lw["g1"], lw["wq"], lw["wkvt"], lw["wg_s"], lw["wkv"]]
    full = lambda shape: pl.BlockSpec(shape, lambda i: (0,) * len(shape))
    shapes = [((n, lw["wq"].shape[1]), BF16), ((4 * GROUP_W, n), F32), ((2 * GROUP_W, n), F32),
              ((n, lw["wg_s"].shape[1]), F32), ((n, lw["wkv"].shape[1]), F32)]
    return pl.pallas_call(
        _qkv_decode_kernel,
        grid=(1,),
        in_specs=[full(x.shape)] + [_const_spec(a.shape) for a in args[1:]],
        out_specs=[full(s) for s, _ in shapes],
        out_shape=[jax.ShapeDtypeStruct(s, t) for s, t in shapes],
        compiler_params=_params(("arbitrary",)), name="qkv_decode")(*args)


def _cmp_topk_kernel(qt_ref, cmp_ref, cmpt_ref, gt_ref, oct_ref, selt_ref, qpad, *, n_sel):
    tq = qt_ref.shape[1]
    nblk = cmp_ref.shape[0]
    t0 = pl.program_id(2) * tq
    kcb = cmp_ref[...].astype(BF16)
    kct = cmpt_ref[...].astype(BF16)
    blk = lax.broadcasted_iota(jnp.int32, (nblk, tq), 0)
    tpos = t0 + lax.broadcasted_iota(jnp.int32, (nblk, tq), 1)
    valid = (blk + 1) * CMP_BLOCK - 1 <= tpos
    gt = gt_ref[...]
    qpad[HEAD_DIM:, :] = jnp.zeros((HEAD_DIM, tq), BF16)
    imp = jnp.zeros((nblk, tq), F32)
    for r in range(Q_PER_KV):
        hs = slice(r * HEAD_DIM, (r + 1) * HEAD_DIM)
        qpad[:HEAD_DIM, :] = qt_ref[hs, :]
        s = _dot(kcb, qpad[...])
        s = jnp.where(valid, s, NEG)
        m = jnp.max(s, axis=0, keepdims=True)
        p = jnp.where(valid, jnp.exp2(s - m), 0.0)
        p = p / jnp.maximum(jnp.sum(p, axis=0, keepdims=True), TINY)
        imp = imp + p
        o = _dot(kct, p.astype(BF16))
        oct_ref[hs, :] = gt[3 * r:3 * r + 1, :] * o[HEAD_DIM:, :]
    cur = lax.shift_right_logical(tpos, 6)
    imp = imp + jnp.where(blk == cur, FORCE, 0.0) + jnp.where(blk == 0, 0.5 * FORCE, 0.0)
    imp = jnp.where(blk <= cur, imp, -FORCE)
    blkf = blk.astype(F32)
    sel = jnp.zeros((nblk, tq), F32)
    for _ in range(n_sel):
        mx = jnp.max(imp, axis=0, keepdims=True)
        first = jnp.min(jnp.where(imp == mx, blkf, float(nblk)), axis=0, keepdims=True)
        hit = blkf == first
        sel = jnp.where(hit, 1.0, sel)
        imp = jnp.where(hit, -jnp.inf, imp)
    selt_ref[...] = jnp.where(blk <= cur, sel, 0.0).astype(BF16)


def _cmp_topk(qt, cmpk, cmpt, gates_t, batch, seq, tq):
    nblk = seq // CMP_BLOCK
    nq = seq // tq
    gw = Q_PER_KV * HEAD_DIM
    n_gate = gates_t.shape[1] // N_KV
    tile = lambda b, g, i: (b, g, i)
    return pl.pallas_call(
        functools.partial(_cmp_topk_kernel, n_sel=min(N_SEL, nblk)),
        grid=(batch, N_KV, nq),
        in_specs=[pl.BlockSpec((None, gw, tq), tile),
                  pl.BlockSpec((nblk, LANES), lambda b, g, i: (b, g)),
                  pl.BlockSpec((None, None, LANES, nblk), lambda b, g, i: (b, g, 0, 0)),
                  pl.BlockSpec((None, n_gate, tq), tile)],
        out_specs=[pl.BlockSpec((None, gw, tq), tile),
                   pl.BlockSpec((None, None, nblk, tq), lambda b, g, i: (g, b, 0, i))],
        out_shape=[jax.ShapeDtypeStruct((batch, N_HEADS * HEAD_DIM, seq), F32),
                   jax.ShapeDtypeStruct((N_KV, batch, nblk, seq), BF16)],
        scratch_shapes=[pltpu.VMEM((LANES, tq), BF16)],
        compiler_params=_params(("arbitrary", "arbitrary", "arbitrary")),
        name="cmp_topk")(qt, cmpk, cmpt, gates_t)


def _flash_kernel(qt_ref, kp_ref, vs_ref, vw_ref, selt_ref, gt_ref, oct_ref, o_ref,
                  qs_ref, qw_ref, bias_ref, s_ref, m_ref, l_ref, acc_ref, out_ref, *, n_back, qc):
    tq = qt_ref.shape[1]
    nblk = selt_ref.shape[0]
    tk = vs_ref.shape[2]
    i = pl.program_id(2)
    t0 = i * tq
    cols = Q_PER_KV * tq
    zeros = jnp.zeros((HEAD_DIM, cols), BF16)
    qs_ref[HEAD_DIM:, :] = zeros
    qw_ref[:HEAD_DIM, :] = zeros
    for r in range(Q_PER_KV):
        qh = qt_ref[r * HEAD_DIM:(r + 1) * HEAD_DIM, :]
        qs_ref[:HEAD_DIM, r * tq:(r + 1) * tq] = qh
        qw_ref[HEAD_DIM:, r * tq:(r + 1) * tq] = qh
    key0 = lax.broadcasted_iota(jnp.int32, (tk, tq), 0)
    tpos = t0 + lax.broadcasted_iota(jnp.int32, (tk, tq), 1)
    e_blk = lax.shift_right_logical(lax.broadcasted_iota(jnp.int32, (tk, nblk), 0), 6)
    e_col = lax.broadcasted_iota(jnp.int32, (tk, nblk), 1)

    def reset():
        m_ref[...] = jnp.full(m_ref.shape, NEG, F32)
        l_ref[...] = jnp.zeros(l_ref.shape, F32)
        acc_ref[...] = jnp.zeros(acc_ref.shape, F32)

    def key_tile(kt):
        return kp_ref[pl.ds(pl.multiple_of(kt * tk, tk), tk), :]

    def fill_scores(q_ref, kt):
        kp = key_tile(kt)
        for c in range(cols // qc):
            b0 = (c * qc) % tq
            s_ref[:, c * qc:(c + 1) * qc] = _dot(kp, q_ref[:, c * qc:(c + 1) * qc]) + bias_ref[:, b0:b0 + qc]

    def attend(q_ref, v_ref, kt, kt_next):
        kp = key_tile(kt_next)
        vt = v_ref[kt]
        pending = None
        for c in range(cols // qc):
            ls = slice(c * qc, (c + 1) * qc)
            b0 = (c * qc) % tq
            s = s_ref[:, ls]
            s_ref[:, ls] = _dot(kp, q_ref[:, ls]) + bias_ref[:, b0:b0 + qc]
            m_old = m_ref[:, ls]
            m_new = jnp.maximum(m_old, jnp.max(s, axis=0, keepdims=True))
            alpha = jnp.exp2(m_old - m_new)
            p = jnp.exp2(s - m_new)
            l_ref[:, ls] = alpha * l_ref[:, ls] + jnp.sum(p, axis=0, keepdims=True)
            m_ref[:, ls] = m_new
            pv = _dot(vt, p.astype(BF16))
            if pending is not None:
                pls, palpha, ppv = pending
                acc_ref[:, pls] = palpha * acc_ref[:, pls] + ppv
            pending = (ls, alpha, pv)
        pls, palpha, ppv = pending
        acc_ref[:, pls] = palpha * acc_ref[:, pls] + ppv

    def emit(k):
        gt = gt_ref[...]
        for r in range(Q_PER_KV):
            ls = slice(r * tq, (r + 1) * tq)
            hs = slice(r * HEAD_DIM, (r + 1) * HEAD_DIM)
            scale = gt[3 * r + k:3 * r + k + 1, :] / l_ref[:, ls]
            out_ref[hs, :] = out_ref[hs, :] + scale * acc_ref[:, ls]

    out_ref[...] = oct_ref[...]

    def sel_bias(kt):
        expand = jnp.where(e_col == kt * (tk // SEL_BLOCK) + e_blk, MASK_BIG, 0.0).astype(BF16)
        chosen = _dot(expand, selt_ref[...])
        key = kt * tk + key0
        bias_ref[...] = jnp.where(key <= tpos, chosen - MASK_BIG, NEG)

    def sel_body(kt, carry):
        kt_next = jnp.minimum(kt + 1, i)
        sel_bias(kt_next)
        attend(qs_ref, vs_ref, kt, kt_next)
        return carry

    reset()
    sel_bias(0)
    fill_scores(qs_ref, 0)
    lax.fori_loop(0, i + 1, sel_body, 0)
    emit(1)

    def win_bias(kt):
        dlt = tpos - (kt * tk + key0)
        bias_ref[...] = jnp.where(dlt >= 0, jnp.where(dlt < WINDOW, 0.0, NEG), NEG)

    n_win = jnp.minimum(i, n_back) + 1

    def win_body(j, carry):
        kt = i - j
        kt_next = i - jnp.minimum(j + 1, n_win - 1)
        win_bias(kt_next)
        attend(qw_ref, vw_ref, kt, kt_next)
        return carry

    reset()
    win_bias(i)
    fill_scores(qw_ref, i)
    lax.fori_loop(0, n_win, win_body, 0)
    emit(2)
    o_ref[...] = out_ref[...].T.astype(BF16)


def _flash(qt, kp, vt, selt, gates_t, oct, batch, seq, tq, qc):
    nblk = seq // SEL_BLOCK
    nq = seq // tq
    nkt, tk = vt.shape[1], vt.shape[3]
    assert tk == tq
    gw = Q_PER_KV * HEAD_DIM
    n_gate = gates_t.shape[1] // N_KV
    n_back = -(-(WINDOW - 1) // tk)
    tile = lambda b, g, i: (b, g, i)
    v_spec = lambda kind: pl.BlockSpec((None, nkt, HEAD_DIM, tk),
                                       lambda b, g, i: (b, 0, kind * N_KV + g, 0))
    cols = Q_PER_KV * tq
    return pl.pallas_call(
        functools.partial(_flash_kernel, n_back=n_back, qc=qc),
        grid=(batch, N_KV, nq),
        in_specs=[pl.BlockSpec((None, gw, tq), tile),
                  pl.BlockSpec((seq, LANES), lambda b, g, i: (b, g)),
                  v_spec(0), v_spec(1),
                  pl.BlockSpec((None, None, nblk, tq), lambda b, g, i: (g, b, 0, i)),
                  pl.BlockSpec((None, n_gate, tq), tile),
                  pl.BlockSpec((None, gw, tq), tile)],
        out_specs=pl.BlockSpec((tq, gw), lambda b, g, i: (b * nq + i, g)),
        out_shape=jax.ShapeDtypeStruct((batch * seq, N_HEADS * HEAD_DIM), BF16),
        scratch_shapes=[pltpu.VMEM((LANES, cols), BF16), pltpu.VMEM((LANES, cols), BF16),
                        pltpu.VMEM((tk, tq), F32), pltpu.VMEM((tk, cols), F32),
                        pltpu.VMEM((1, cols), F32), pltpu.VMEM((1, cols), F32),
                        pltpu.VMEM((HEAD_DIM, cols), F32), pltpu.VMEM((gw, tq), F32)],
        compiler_params=_params(("arbitrary", "arbitrary", "arbitrary")),
        name="flash_sel_win")(qt, kp, vt, vt, selt, gates_t, oct)


def _shift_rows(x, d, fill):
    row = lax.broadcasted_iota(jnp.int32, x.shape, 0)
    return jnp.where(row < d, fill, pltpu.roll(x, d, axis=0))


def _lagged(u, tail, lag):
    us = pltpu.roll(u, lag, axis=0)
    row = lax.broadcasted_iota(jnp.int32, tail.shape, 0)
    top = jnp.where(row < lag, pltpu.roll(tail, lag, axis=0), us[:SUBLANES])
    return jnp.concatenate([top, us[SUBLANES:]], axis=0)


def _mix_tail(x, h, onsa, v_sc, v_lru, wgm_ref, wsco_ref, wlo_ref, wno_ref, wo_ref):
    d = x.shape[1]
    u_sc = _dot(v_sc.astype(BF16), wsco_ref[...])
    u_lru = _dot(v_lru.astype(BF16), wlo_ref[...])
    u_nsa = _dot(onsa, wno_ref[...])
    mix = jax.nn.sigmoid(_dot(h, wgm_ref[:, :d])) * u_nsa
    mix = mix + jax.nn.sigmoid(_dot(h, wgm_ref[:, d:2 * d])) * u_sc
    mix = mix + jax.nn.sigmoid(_dot(h, wgm_ref[:, 2 * d:])) * u_lru
    return x + _dot(mix.astype(BF16), wo_ref[...])


def _lru_coeffs(xc, wgate_ref, bgate_ref, lam_ref):
    w = xc.shape[1]
    gates = jax.nn.sigmoid(_dot(xc.astype(BF16), wgate_ref[...]) + bgate_ref[...])
    lam = -lam_ref[...]
    softplus = jnp.maximum(lam, 0.0) + jnp.log1p(jnp.exp(-jnp.abs(lam)))
    log_a = -LRU_C * gates[:, :w] * softplus
    a = jnp.exp(log_a)
    th = jnp.tanh(log_a)
    mult = jnp.sqrt(-2.0 * th / (1.0 - th))
    return a, mult, gates[:, w:]


def _mix_prompt_kernel(x_ref, onsa_ref, g_ref, wb_ref, wgm_ref, wscc_ref, wlc_ref, blc_ref,
                       wgate_ref, bgate_ref, lam_ref, wsco_ref, wlo_ref, wno_ref, wo_ref,
                       x1_ref, sct_ref, lct_ref, hl_ref, sc_tail, lc_tail, h_prev):
    t = pl.program_id(1)
    tm = x_ref.shape[0]
    w = sc_tail.shape[1]

    @pl.when(t == 0)
    def _():
        sc_tail[...] = jnp.zeros_like(sc_tail)
        lc_tail[...] = jnp.zeros_like(lc_tail)
        h_prev[...] = jnp.zeros_like(h_prev)

    x = x_ref[...]
    h = _rms(x, g_ref[...]).astype(BF16)
    proj = _dot(h, wb_ref[...])
    b_sc, c_sc, x_sc = proj[:, :w], proj[:, w:2 * w], proj[:, 2 * w:3 * w]
    x_lru, y_lru = proj[:, 3 * w:4 * w], proj[:, 4 * w:]

    cx = c_sc * x_sc
    wscc = wscc_ref[...]
    tail = sc_tail[...]
    conv = cx * wscc[SC_K - 1:SC_K]
    for lag in range(1, SC_K):
        conv = conv + _lagged(cx, tail, lag) * wscc[SC_K - 1 - lag:SC_K - lag]
    sc_tail[...] = cx[tm - SUBLANES:]
    sct_ref[...] = cx[tm - SUBLANES:]
    v_sc = b_sc * conv

    wlc = wlc_ref[...]
    tail = lc_tail[...]
    xc = x_lru * wlc[LRU_K - 1:LRU_K]
    for lag in range(1, LRU_K):
        xc = xc + _lagged(x_lru, tail, lag) * wlc[LRU_K - 1 - lag:LRU_K - lag]
    xc = xc + blc_ref[...]
    lc_tail[...] = x_lru[tm - SUBLANES:]
    lct_ref[...] = x_lru[tm - SUBLANES:]

    a, mult, gate_i = _lru_coeffs(xc, wgate_ref, bgate_ref, lam_ref)
    reset = (t * tm + lax.broadcasted_iota(jnp.int32, a.shape, 0)) == 0
    a = jnp.where(reset, 0.0, a)
    mult = jnp.where(reset, 1.0, mult)
    bx = mult * gate_i * xc
    d = 1
    while d < tm:
        bx = a * _shift_rows(bx, d, 0.0) + bx
        a = a * _shift_rows(a, d, 1.0)
        d *= 2
    hs = a * h_prev[...] + bx
    h_prev[...] = hs[tm - 1:]
    hl_ref[...] = hs[tm - SUBLANES:]
    v_lru = hs * jax.nn.gelu(y_lru)

    x1_ref[...] = _mix_tail(x, h, onsa_ref[...], v_sc, v_lru, wgm_ref, wsco_ref, wlo_ref, wno_ref, wo_ref)


def _mix_decode_kernel(x_ref, onsa_ref, scp_ref, lcp_ref, h0_ref, g_ref, wb_ref, wgm_ref, wscc_ref,
                       wlc_ref, blc_ref, wgate_ref, bgate_ref, lam_ref, wsco_ref, wlo_ref, wno_ref,
                       wo_ref, x1_ref, cx_ref, xl_ref, hn_ref):
    w = h0_ref.shape[1]
    x = x_ref[...]
    h = _rms(x, g_ref[...]).astype(BF16)
    proj = _dot(h, wb_ref[...])
    b_sc, c_sc, x_sc = proj[:, :w], proj[:, w:2 * w], proj[:, 2 * w:3 * w]
    x_lru, y_lru = proj[:, 3 * w:4 * w], proj[:, 4 * w:]

    cx = c_sc * x_sc
    wscc = wscc_ref[...]
    conv = cx * wscc[SC_K - 1:SC_K]
    for k in range(SC_K - 1):
        conv = conv + scp_ref[:, k * w:(k + 1) * w] * wscc[k:k + 1]
    cx_ref[...] = cx
    v_sc = b_sc * conv

    wlc = wlc_ref[...]
    xc = x_lru * wlc[LRU_K - 1:LRU_K]
    for k in range(LRU_K - 1):
        xc = xc + lcp_ref[:, k * w:(k + 1) * w] * wlc[k:k + 1]
    xc = xc + blc_ref[...]
    xl_ref[...] = x_lru

    a, mult, gate_i = _lru_coeffs(xc, wgate_ref, bgate_ref, lam_ref)
    hs = a * h0_ref[...] + mult * gate_i * xc
    hn_ref[...] = hs
    v_lru = hs * jax.nn.gelu(y_lru)

    x1_ref[...] = _mix_tail(x, h, onsa_ref[...], v_sc, v_lru, wgm_ref, wsco_ref, wlo_ref, wno_ref, wo_ref)


def _mix_weight_args(lw):
    return [lw["g1"], lw["wb"], lw["wgm"], lw["wscc"], lw["wlc"], lw["blc"], lw["wgate"], lw["bgate"],
            lw["lam"], lw["wsco"], lw["wlo"], lw["wno"], lw["wo"]]


def _mix_prompt(x, onsa, lw, batch, seq, tm):
    n, d = x.shape
    w = lw["wsco"].shape[0]
    nt = seq // tm
    row = lambda b, t: (b * nt + t, 0)
    per_b = lambda b, t: (b, 0, 0)
    wargs = _mix_weight_args(lw)
    tail_shape = jax.ShapeDtypeStruct((batch, SUBLANES, w), F32)
    tail_spec = pl.BlockSpec((None, SUBLANES, w), per_b)
    return pl.pallas_call(
        _mix_prompt_kernel,
        grid=(batch, nt),
        in_specs=[pl.BlockSpec((tm, d), row), pl.BlockSpec((tm, onsa.shape[1]), row)]
        + [_const_spec(a.shape) for a in wargs],
        out_specs=[pl.BlockSpec((tm, d), row), tail_spec, tail_spec, tail_spec],
        out_shape=[jax.ShapeDtypeStruct((n, d), F32), tail_shape, tail_shape, tail_shape],
        scratch_shapes=[pltpu.VMEM((SUBLANES, w), F32), pltpu.VMEM((SUBLANES, w), F32),
                        pltpu.VMEM((1, w), F32)],
        compiler_params=_params(("arbitrary", "arbitrary")),
        name="mix_prompt")(x, onsa, *wargs)


def _mix_decode(x, onsa, sc_prev, lc_prev, h0, lw):
    n, d = x.shape
    w = h0.shape[1]
    wargs = _mix_weight_args(lw)
    full = lambda a: pl.BlockSpec(a.shape, lambda i: (0,) * a.ndim)
    acts = [x, onsa, sc_prev, lc_prev, h0]
    state = jax.ShapeDtypeStruct((n, w), F32)
    return pl.pallas_call(
        _mix_decode_kernel,
        grid=(1,),
        in_specs=[full(a) for a in acts] + [_const_spec(a.shape) for a in wargs],
        out_specs=[pl.BlockSpec((n, d), lambda i: (0, 0))] + [pl.BlockSpec((n, w), lambda i: (0, 0))] * 3,
        out_shape=[jax.ShapeDtypeStruct((n, d), F32), state, state, state],
        compiler_params=_params(("arbitrary",)),
        name="mix_decode")(*acts, *wargs)


def _ffn_kernel(*refs, final, n_chunks):
    if final:
        x_ref, g_ref, wi_ref, wout_ref, gf_ref, y_ref = refs
    else:
        x_ref, g_ref, wi_ref, wout_ref, y_ref = refs
    x = x_ref[...]
    h = _rms(x, g_ref[...]).astype(BF16)
    d_ff = wout_ref.shape[0]
    ck = d_ff // n_chunks
    y = x
    for c in range(n_chunks):
        gate = _dot(h, wi_ref[:, c * ck:(c + 1) * ck])
        up = _dot(h, wi_ref[:, d_ff + c * ck:d_ff + (c + 1) * ck])
        y = y + _dot((jax.nn.silu(gate) * up).astype(BF16), wout_ref[c * ck:(c + 1) * ck, :])
    if final:
        y = _rms(y, gf_ref[...])
    y_ref[...] = y


def _ffn(x, g, wi, wout, gf, tm):
    n, d = x.shape
    final = gf is not None
    d_ff = wout.shape[0]
    n_chunks = 2 if d_ff % (2 * LANES) == 0 else 1
    row = lambda i: (i, 0)
    args = [x, g, wi, wout] + ([gf] if final else [])
    return pl.pallas_call(
        functools.partial(_ffn_kernel, final=final, n_chunks=n_chunks),
        grid=(n // tm,),
        in_specs=[pl.BlockSpec((tm, d), row)] + [_const_spec(a.shape) for a in args[1:]],
        out_specs=pl.BlockSpec((tm, d), row),
        out_shape=jax.ShapeDtypeStruct((n, d), F32),
        compiler_params=_params(("arbitrary",)),
        name="ffn")(*args)


def _group_onehot(rows, cols, row_div, col_div):
    r = lax.broadcasted_iota(jnp.int32, (rows, cols), 0) // row_div
    c = lax.broadcasted_iota(jnp.int32, (rows, cols), 1) // col_div
    return r == c


def _tile_lanes(q, reps):
    hd = q.shape[1]
    r = lax.broadcasted_iota(jnp.int32, (hd, reps * hd), 0)
    c = lax.broadcasted_iota(jnp.int32, (hd, reps * hd), 1)
    rep = jnp.where(c % hd == r, 1.0, 0.0).astype(BF16)
    return _dot(q, rep)


def _fold_heads(o):
    own = _group_onehot(o.shape[0], o.shape[1], Q_PER_KV, HEAD_DIM)
    o = jnp.where(own, o, 0.0)
    out = o[:, :HEAD_DIM]
    for g in range(1, o.shape[1] // HEAD_DIM):
        out = out + o[:, g * HEAD_DIM:(g + 1) * HEAD_DIM]
    return out


def _dec_cmp_kernel(pt_ref, q_ref, gate_ref, wct_ref, seg_ref, cache_ref, oc_ref, idx_ref, buf, sem,
                    *, row0, n_pages, page, past, n_sel):
    step = pl.program_id(0)
    spb = q_ref.shape[0]

    def copies(st, slot):
        out = []
        for u in range(spb):
            for pg in range(n_pages):
                base = (row0 + pt_ref[(st * spb + u) * n_pages + pg]) * 4
                for kind in range(2):
                    out.append(pltpu.make_async_copy(cache_ref.at[base + kind],
                                                     buf.at[slot, u, kind, :, :, pl.ds(pg * page, page)],
                                                     sem.at[slot]))
        return out

    slot = step % 2

    @pl.when(step == 0)
    def _():
        for c in copies(0, 0):
            c.start()

    @pl.when(step + 1 < pl.num_programs(0))
    def _():
        for c in copies(step + 1, 1 - slot):
            c.start()

    for c in copies(step, slot):
        c.wait()

    for u in range(spb):
        _dec_cmp_sample(q_ref.at[u], gate_ref.at[u], wct_ref, seg_ref, oc_ref.at[u], idx_ref.at[u],
                        buf.at[slot, u], n_pages=n_pages, page=page, past=past, n_sel=n_sel)


def _dec_cmp_sample(q_ref, gate_ref, wct_ref, seg_ref, oc_ref, idx_ref, buf, *, n_pages, page, past, n_sel):
    npad = seg_ref.shape[1]
    kchunk = 4 * page
    cmp_t = []
    for kind in range(2):
        w = jnp.concatenate([wct_ref[kind]] * (kchunk // page), axis=1)
        acc = jnp.zeros((GROUP_W, npad), F32)
        for c in range(past // kchunk):
            cs = slice(c * kchunk, (c + 1) * kchunk)
            y = buf[kind, :, :, cs].reshape(GROUP_W, kchunk) * w
            hi = y.astype(BF16)
            lo = (y - hi.astype(F32)).astype(BF16)
            seg = seg_ref[cs, :]
            acc = acc + _dot(hi, seg) + _dot(lo, seg)
        cmp_t.append(acc.astype(BF16))
    kc_t, vc_t = cmp_t

    q = q_ref[...]
    own = _group_onehot(N_HEADS, GROUP_W, Q_PER_KV, HEAD_DIM)
    qbd = jnp.where(own, _tile_lanes(q, N_KV), 0.0).astype(BF16)
    s = _dot(qbd, kc_t)
    blk = lax.broadcasted_iota(jnp.int32, (N_HEADS, npad), 1)
    valid = (blk + 1) * CMP_BLOCK - 1 <= past
    s = jnp.where(valid, s, NEG)
    m = jnp.max(s, axis=-1, keepdims=True)
    p = jnp.where(valid, jnp.exp(s - m), 0.0)
    p = p / jnp.maximum(jnp.sum(p, axis=-1, keepdims=True), TINY)
    o = _fold_heads(_dot_nt(p.astype(BF16), vc_t))
    oc_ref[...] = gate_ref[:, 0:1] * o

    blk8 = lax.broadcasted_iota(jnp.int32, (SUBLANES, npad), 1)
    row8 = lax.broadcasted_iota(jnp.int32, (SUBLANES, npad), 0)
    imp = jnp.zeros((SUBLANES, npad), F32)
    for g in range(N_KV):
        psum = jnp.sum(p[g * Q_PER_KV:(g + 1) * Q_PER_KV], axis=0, keepdims=True)
        imp = jnp.where(row8 == g, psum, imp)
    cur = past // SEL_BLOCK
    imp = imp + jnp.where(blk8 == cur, FORCE, 0.0) + jnp.where(blk8 == 0, 0.5 * FORCE, 0.0)
    imp = jnp.where(blk8 <= cur, imp, -jnp.inf)
    blkf = blk8.astype(F32)
    idx = jnp.zeros((SUBLANES, LANES), F32)
    lane = lax.broadcasted_iota(jnp.int32, (SUBLANES, LANES), 1)
    for j in range(n_sel):
        mx = jnp.max(imp, axis=-1, keepdims=True)
        first = jnp.min(jnp.where(imp == mx, blkf, float(npad)), axis=-1, keepdims=True)
        idx = jnp.where(lane == j, first, idx)
        imp = jnp.where(blkf == first, -jnp.inf, imp)
    idx_ref[...] = idx.astype(jnp.int32)


def _dec_cmp(page_table, q3, gates3, wct, cache4, row0, past):
    db = q3.shape[0]
    n_pages = page_table.shape[1]
    page = cache4.shape[3]
    n_sel = min(N_SEL, past // SEL_BLOCK + 1)
    npad = -(-(past // CMP_BLOCK + 1) // LANES) * LANES
    seg = (jnp.arange(past)[:, None] // CMP_BLOCK == jnp.arange(npad)[None, :]).astype(BF16)
    spb = DEC_SAMPLES_PER_STEP if db % DEC_SAMPLES_PER_STEP == 0 else 1
    per_b = lambda b, pt: (b, 0, 0)
    grid_spec = pltpu.PrefetchScalarGridSpec(
        num_scalar_prefetch=1,
        grid=(db // spb,),
        in_specs=[pl.BlockSpec((spb, N_HEADS, HEAD_DIM), per_b),
                  pl.BlockSpec((spb, N_HEADS, LANES), per_b),
                  pl.BlockSpec(wct.shape, lambda b, pt: (0, 0, 0)),
                  pl.BlockSpec(seg.shape, lambda b, pt: (0, 0)),
                  pl.BlockSpec(memory_space=pl.ANY)],
        out_specs=[pl.BlockSpec((spb, N_HEADS, HEAD_DIM), per_b),
                   pl.BlockSpec((spb, SUBLANES, LANES), per_b)],
        scratch_shapes=[pltpu.VMEM((2, spb, 2, N_KV, HEAD_DIM, n_pages * page), F32),
                        pltpu.SemaphoreType.DMA((2,))])
    return pl.pallas_call(
        functools.partial(_dec_cmp_kernel, row0=row0, n_pages=n_pages, page=page, past=past, n_sel=n_sel),
        grid_spec=grid_spec,
        out_shape=[jax.ShapeDtypeStruct((db, N_HEADS, HEAD_DIM), F32),
                   jax.ShapeDtypeStruct((db, SUBLANES, LANES), jnp.int32)],
        compiler_params=_params(("arbitrary",)),
        name="dec_cmp_topk")(page_table.reshape(-1), q3, gates3, wct, seg, cache4)


def _dec_attn_kernel(idx_ref, pt_ref, q_ref, gate_ref, oc_ref, kvn_ref, kwcol_ref, win_ref, cache_ref,
                     *rest, row0, n_sel, past, page):
    o_ref, wout_ref, kvbuf, sem = rest[-4:]
    step = pl.program_id(0)
    spb = q_ref.shape[0]
    nblk_past = past // SEL_BLOCK
    per_page = page // SEL_BLOCK
    n_pages = nblk_past // per_page

    def block_copies(bb, slot, u, g, j, n):
        base = (row0 + pt_ref[bb * n_pages + n // per_page]) * 4
        dst = pl.ds(j * page, page)
        return pltpu.make_async_copy(cache_ref.at[pl.ds(base + 2, 2), g], kvbuf.at[slot, u, g, :, :, dst],
                                     sem.at[slot])

    def gather(st, slot, wait):
        for u in range(spb):
            bb = st * spb + u
            for g in range(N_KV):
                for j in range(n_sel):
                    n = jnp.minimum(idx_ref[(bb * N_KV + g) * n_sel + j], nblk_past - 1)
                    c = block_copies(bb, slot, u, g, j, n)
                    c.wait() if wait else c.start()

    slot = step % 2

    @pl.when(step == 0)
    def _():
        gather(0, 0, wait=False)

    @pl.when(step + 1 < pl.num_programs(0))
    def _():
        gather(step + 1, 1 - slot, wait=False)

    gather(step, slot, wait=True)
    for u in range(spb):
        _dec_attn_sample(step * spb + u, idx_ref, q_ref.at[u], gate_ref.at[u], oc_ref.at[u], kvn_ref.at[u],
                         kwcol_ref, win_ref.at[u], o_ref.at[u], wout_ref.at[u], kvbuf.at[slot, u],
                         n_sel=n_sel, past=past, page=page)


def _dec_attn_sample(b, idx_ref, q_ref, gate_ref, oc_ref, kvn_ref, kwcol_ref, win_ref, o_ref, wout_ref,
                     kvbuf, *, n_sel, past, page):
    nblk_past = past // SEL_BLOCK
    per_page = page // SEL_BLOCK
    win_len = win_ref.shape[1]
    q = q_ref[...]
    own = _group_onehot(N_HEADS, GROUP_W, Q_PER_KV, HEAD_DIM)
    qbd_f = jnp.where(own, _tile_lanes(q, N_KV), 0.0)
    qbd = qbd_f.astype(BF16)
    gt = gate_ref[...]
    kvn = kvn_ref[...]

    win = win_ref[...]
    s = _dot(qbd, win[:GROUP_W].astype(BF16))
    colw = lax.broadcasted_iota(jnp.int32, s.shape, 1)
    s = jnp.where(win_len - colw < WINDOW, s, NEG)
    s_new = jnp.sum(qbd_f * kvn[:, 4 * GROUP_W:5 * GROUP_W], axis=-1, keepdims=True)
    m = jnp.maximum(jnp.max(s, axis=-1, keepdims=True), s_new)
    p = jnp.exp(s - m)
    p_new = jnp.exp(s_new - m)
    l = jnp.sum(p, axis=-1, keepdims=True) + p_new
    o_w = _fold_heads(_dot_nt(p.astype(BF16), win[GROUP_W:].astype(BF16))
                      + p_new * kvn[:, 5 * GROUP_W:]) / l

    lane_b = lax.broadcasted_iota(jnp.int32, kwcol_ref.shape, 1)
    new_col = jnp.sum(jnp.where(lane_b == b, kwcol_ref[...], 0.0), axis=1, keepdims=True)
    pos = lax.broadcasted_iota(jnp.int32, win.shape, 1)
    wout_ref[...] = jnp.where(pos == win_len - 1, new_col, pltpu.roll(win, win_len - 1, axis=1))

    nkeys = n_sel * page
    rowg = lax.broadcasted_iota(jnp.int32, (N_HEADS, HEAD_DIM), 0) // Q_PER_KV
    rows = lax.broadcasted_iota(jnp.int32, (N_HEADS, nkeys), 0) // Q_PER_KV
    lanes = lax.broadcasted_iota(jnp.int32, (N_HEADS, nkeys), 1)
    colj = lanes // page
    in_page = (lanes % page) // SEL_BLOCK
    qf = q.astype(F32)
    s = jnp.zeros((N_HEADS, nkeys), F32)
    okay = jnp.zeros((N_HEADS, nkeys), F32)
    for g in range(N_KV):
        qg = jnp.where(rowg == g, qf, 0.0).astype(BF16)
        s = s + _dot(qg, kvbuf[g, 0].astype(BF16))
        for j in range(n_sel):
            n = idx_ref[(b * N_KV + g) * n_sel + j]
            flag = jnp.where(n < nblk_past, 1.0, 0.0)
            here = jnp.where(in_page == n % per_page, flag, 0.0)
            okay = jnp.where(rows == g, jnp.where(colj == j, here, okay), okay)
    s = jnp.where(okay > 0.5, s, NEG)
    s_new = jnp.sum(qbd_f * kvn[:, 2 * GROUP_W:3 * GROUP_W], axis=-1, keepdims=True)
    m = jnp.maximum(jnp.max(s, axis=-1, keepdims=True), s_new)
    p = jnp.where(okay > 0.5, jnp.exp(s - m), 0.0)
    p_new = jnp.exp(s_new - m)
    l = jnp.sum(p, axis=-1, keepdims=True) + p_new
    o_sel = jnp.zeros((N_HEADS, HEAD_DIM), F32)
    for g in range(N_KV):
        pg = jnp.where(rows == g, p, 0.0).astype(BF16)
        o_sel = o_sel + _dot_nt(pg, kvbuf[g, 1].astype(BF16))
    o_s = (o_sel + _fold_heads(p_new * jnp.where(own, kvn[:, 3 * GROUP_W:4 * GROUP_W], 0.0))) / l

    o_ref[...] = (oc_ref[...] + gt[:, 1:2] * o_s + gt[:, 2:3] * o_w).astype(BF16)


def _dec_attn(idx, page_table, q3, gates3, oc, kv_new, kw_cols, win, cache4, row0, win_row0, past, win_all):
    db = q3.shape[0]
    page = cache4.shape[3]
    n_sel = idx.shape[-1]
    spb = DEC_SAMPLES_PER_STEP if db % DEC_SAMPLES_PER_STEP == 0 else 1
    operands = [idx.reshape(-1), page_table.reshape(-1), q3, gates3, oc, kv_new[:, None, :], kw_cols, win,
                cache4, win_all]
    per_b = lambda b, *_: (b, 0, 0)
    win_b = lambda b, *_: (win_row0 // spb + b, 0, 0)
    win_block = (spb,) + win.shape[1:]
    grid_spec = pltpu.PrefetchScalarGridSpec(
        num_scalar_prefetch=2,
        grid=(db // spb,),
        in_specs=[pl.BlockSpec((spb, N_HEADS, HEAD_DIM), per_b),
                  pl.BlockSpec((spb, N_HEADS, LANES), per_b),
                  pl.BlockSpec((spb, N_HEADS, HEAD_DIM), per_b),
                  pl.BlockSpec((spb, 1, kv_new.shape[-1]), per_b),
                  pl.BlockSpec(kw_cols.shape, lambda b, *_: (0, 0)),
                  pl.BlockSpec(win_block, win_b),
                  pl.BlockSpec(memory_space=pl.ANY), pl.BlockSpec(memory_space=pl.ANY)],
        out_specs=[pl.BlockSpec((spb, N_HEADS, HEAD_DIM), per_b),
                   pl.BlockSpec(win_block, win_b)],
        scratch_shapes=[pltpu.VMEM((2, spb, N_KV, 2, HEAD_DIM, n_sel * page), F32),
                        pltpu.SemaphoreType.DMA((2,))])
    return pl.pallas_call(
        functools.partial(_dec_attn_kernel, row0=row0, n_sel=n_sel, past=past, page=page),
        grid_spec=grid_spec,
        out_shape=[jax.ShapeDtypeStruct((db, N_HEADS, HEAD_DIM), BF16),
                   jax.ShapeDtypeStruct(win.shape, F32)],
        input_output_aliases={len(operands) - 1: 1},
        compiler_params=_params(("arbitrary",)),
        name="dec_attn")(*operands)


def _layer_weights(l, norm1_g, w_in, w_cmp_k, w_cmp_v, w_nsa_out, w_sc_conv, w_sc_out, w_lru_conv,
                   b_lru_conv, w_lru_gate, b_lru_gate, lru_lambda, w_lru_out, w_o, norm2_g, w_ffn_in,
                   w_ffn_out):
    d = w_in.shape[1]
    nsa = N_HEADS * HEAD_DIM
    kvc = 2 * GROUP_W
    sc_w = w_sc_out.shape[1]
    lru_w = w_lru_out.shape[1]
    wi = w_in[l]
    o = 0
    wq = wi[:, o:o + nsa]; o += nsa
    wkv = wi[:, o:o + 3 * kvc]; o += 3 * kvc
    wg = wi[:, o:o + 3 * N_HEADS]; o += 3 * N_HEADS
    wb = wi[:, o:o + 3 * sc_w + 2 * lru_w]; o += 3 * sc_w + 2 * lru_w
    wgm = wi[:, o:]
    wkc = wkv[:, :kvc].reshape(d, 2, N_KV, HEAD_DIM).transpose(0, 2, 1, 3).reshape(d, kvc)
    k_slc = wkv[:, kvc:kvc + GROUP_W].reshape(d, N_KV, 1, HEAD_DIM)
    k_win = wkv[:, 2 * kvc:2 * kvc + GROUP_W].reshape(d, N_KV, 1, HEAD_DIM)
    wkp = jnp.concatenate([k_slc, k_win], axis=2).reshape(d, 2 * GROUP_W)
    pad_g = ((0, 0), (0, 0), (0, 2 * SUBLANES - 3 * Q_PER_KV))
    wgt = jnp.pad(wg.reshape(d, N_KV, 3 * Q_PER_KV), pad_g).reshape(d, N_KV * 2 * SUBLANES).T
    pad_h = ((0, 0), (0, 0), (0, LANES - 3))
    wg_s = jnp.pad(wg.reshape(d, N_HEADS, 3), pad_h).reshape(d, N_HEADS * LANES)
    ck, cv = w_cmp_k[l], w_cmp_v[l]
    wc_p = jnp.tile(jnp.concatenate([ck, cv], axis=1), (1, N_KV))
    wct = jnp.stack([jnp.tile(ck.T, (N_KV, 2)), jnp.tile(cv.T, (N_KV, 2))])
    eye = jnp.eye(LRU_HEADS, dtype=F32)
    wgate = jnp.concatenate(
        [(eye[:, None, :, None] * w_lru_gate[l, k][:, :, None, :]).reshape(lru_w, lru_w) for k in range(2)],
        axis=1)
    bf = lambda a: a.astype(BF16)
    return {
        "g1": norm1_g[l][None], "wq": bf(wq), "wqt": bf(wq.T), "wkvt": bf(wkv.T), "wkv": bf(wkv),
        "wkc": bf(wkc), "wkp": bf(wkp), "wgt": bf(wgt), "wg_s": bf(wg_s), "wc_p": wc_p, "wct": wct,
        "wb": bf(wb), "wgm": bf(wgm), "wscc": w_sc_conv[l], "wlc": w_lru_conv[l],
        "blc": b_lru_conv[l][None], "wgate": bf(wgate), "bgate": b_lru_gate[l].reshape(1, 2 * lru_w),
        "lam": lru_lambda[l][None], "wsco": bf(w_sc_out[l]), "wlo": bf(w_lru_out[l]),
        "wno": bf(w_nsa_out[l]), "wo": bf(w_o[l]), "g2": norm2_g[l][None],
        "wfi": bf(w_ffn_in[l]), "wfo": bf(w_ffn_out[l]),
    }


def _pick_tile(n, pref):
    t = min(n, pref)
    while n % t:
        t //= 2
    return t


def kernel(x_prompt, x_sample, cache_kv, cache_win, state_sconv, state_lru_conv, state_lru_h, page_table, norm1_g, w_in, w_cmp_k, w_cmp_v, w_nsa_out, w_sc_conv, w_sc_out, w_lru_conv, b_lru_conv, w_lru_gate, b_lru_gate, lru_lambda, w_lru_out, w_o, norm2_g, w_ffn_in, w_ffn_out, final_g):
    batch, seq, d = x_prompt.shape
    db, dec_seq, _ = x_sample.shape
    depth = w_in.shape[0]
    n_pool, page = cache_kv.shape[1], cache_kv.shape[2]
    past = page_table.shape[1] * page
    win_len = cache_win.shape[2]
    assert dec_seq == 1 and past % SEL_BLOCK == 0 and page % SEL_BLOCK == 0
    assert seq % 256 == 0 and page == LANES

    tq = 256
    tm_qkv = _pick_tile(seq, 512)
    tm_mix = _pick_tile(seq, 256)
    tm_ffn = _pick_tile(batch * seq, 512)
    gf = final_g[None]

    cache4 = jnp.transpose(cache_kv, (0, 1, 3, 4, 5, 2)).reshape(depth * n_pool * 4, N_KV, HEAD_DIM, page)
    win_t = jnp.transpose(cache_win, (0, 1, 3, 4, 5, 2)).reshape(depth * db, 2 * GROUP_W, win_len)

    xp = x_prompt.reshape(batch * seq, d)
    xs = x_sample.reshape(db, d)
    outs_p, outs_s = [], []
    kv_all = jnp.zeros((depth, batch, 4 * GROUP_W, seq), F32)
    win_all = jnp.zeros(win_t.shape, F32)
    for l in range(depth):
        lw = _layer_weights(l, norm1_g, w_in, w_cmp_k, w_cmp_v, w_nsa_out, w_sc_conv, w_sc_out,
                            w_lru_conv, b_lru_conv, w_lru_gate, b_lru_gate, lru_lambda, w_lru_out,
                            w_o, norm2_g, w_ffn_in, w_ffn_out)
        last = l == depth - 1
        qt, kv_all, kwt, gates_t, kp, vt, cmpk = _qkv_prompt(xp, lw, batch, seq, tm_qkv, tq, l, depth, kv_all)
        cmpt = cmpk.reshape(batch, seq // CMP_BLOCK, N_KV, LANES).transpose(0, 2, 3, 1)
        oct, selt = _cmp_topk(qt, cmpk, cmpt, gates_t, batch, seq, _pick_tile(seq, 2 * tq))
        onsa = _flash(qt, kp, vt, selt, gates_t, oct, batch, seq, tq, qc=LANES)
        x1, sct, lct, hl = _mix_prompt(xp, onsa, lw, batch, seq, tm_mix)
        xp = _ffn(x1, lw["g2"], lw["wfi"], lw["wfo"], gf if last else None, tm_ffn)
        w_keep = min(WINDOW, seq)
        outs_p.append((None, kwt[:, :, seq - w_keep:], sct[:, SUBLANES - (SC_K - 1):],
                       lct[:, SUBLANES - (LRU_K - 1):], hl[:, SUBLANES - 1]))
        q, kvt, kwt, gates, kvr = _qkv_decode(xs, lw)
        q3 = q.reshape(db, N_HEADS, HEAD_DIM)
        gates3 = gates.reshape(db, N_HEADS, LANES)
        oc, idx = _dec_cmp(page_table, q3, gates3, lw["wct"], cache4, l * n_pool, past)
        n_sel = min(N_SEL, past // SEL_BLOCK + 1)
        idx = idx[:, :N_KV, :n_sel]
        onsa, win_all = _dec_attn(idx, page_table, q3, gates3, oc, kvr, kwt, win_t, cache4,
                                  l * n_pool, l * db, past, win_all)
        sc_prev = state_sconv[l].reshape(db, -1)
        lc_prev = state_lru_conv[l].reshape(db, -1)
        x1, cx, xl, hn = _mix_decode(xs, onsa.reshape(db, N_HEADS * HEAD_DIM), sc_prev, lc_prev,
                                     state_lru_h[l], lw)
        xs = _ffn(x1, lw["g2"], lw["wfi"], lw["wfo"], gf if last else None, db)
        outs_s.append((kvt, None,
                       jnp.concatenate([state_sconv[l][:, 1:], cx[:, None]], axis=1),
                       jnp.concatenate([state_lru_conv[l][:, 1:], xl[:, None]], axis=1),
                       hn))
    stack = lambda outs, i: jnp.stack([o[i] for o in outs])
    rows_out = lambda a, kinds: jnp.transpose(
        a.reshape(a.shape[:2] + (kinds, N_KV, HEAD_DIM, a.shape[-1])), (0, 1, 5, 2, 3, 4))
    y_prompt = xp.reshape(batch, seq, d)
    y_sample = xs.reshape(db, 1, d)
    kv_s = jnp.transpose(stack(outs_s, 0).reshape(depth, 1, 4, N_KV, HEAD_DIM, db), (0, 5, 1, 2, 3, 4))
    return (y_prompt, y_sample,
            rows_out(kv_all, 4), rows_out(stack(outs_p, 1), 2),
            stack(outs_p, 2), stack(outs_p, 3), stack(outs_p, 4),
            kv_s, rows_out(win_all.reshape((depth, db) + win_all.shape[1:]), 2),
            stack(outs_s, 2), stack(outs_s, 3), stack(outs_s, 4))
```
